```python
import math
import jax, jax.numpy as jnp
from jax import lax
import numpy as np

D_MODEL = 1024
BATCH = 1
SEQ = 16384
DEPTH = 2

N_MIXERS = 2
N_A = (DEPTH + 1) // N_MIXERS
N_B = DEPTH // N_MIXERS

S5_WIDTH = D_MODEL
S5_GROUP = 16
S5_GROUPS = S5_WIDTH // S5_GROUP
S5_STATE = 64
S5_CHUNK = 128
S5_DT_MIN = 1e-3
S5_DT_MAX = 1e-1

N_HEADS = 8
HEAD_DIM = D_MODEL // N_HEADS
ATTN_WIDTH = N_HEADS * HEAD_DIM
MOBA_BLOCK = 256
MOBA_TOPK = 3
Q_BLOCK = 128
ROPE_THETA = 500000.0
ROT_DIM = HEAD_DIM // 4

D_FF = 2816
MACARON_WEIGHT = 0.5
EPS = 1e-6

kernel_name = 'hybrid_s5_moba_macaron'


def rms_norm(x, g):
    xf = x.astype(jnp.float32)
    y = xf * lax.rsqrt(jnp.mean(xf * xf, axis=-1, keepdims=True) + EPS)
    return (y * g.astype(jnp.float32)).astype(x.dtype)


def swiglu(h, w_gate, w_up, w_down):
    return (jax.nn.silu(h @ w_gate) * (h @ w_up)) @ w_down


def cmul(ar, ai, br, bi):
    return ar * br - ai * bi, ar * bi + ai * br


def s5_mixer(h, w_in, a_re, a_im, log_dt, b_re, b_im, c_re, c_im, d_skip, w_glu, w_out):
    f32 = jnp.float32
    bsz, seq, _ = h.shape
    u = (h @ w_in).astype(f32)
    dt = jnp.exp(log_dt.astype(f32))[:, None]
    a_re = a_re.astype(f32)
    a_im = a_im.astype(f32)
    mag = jnp.exp(a_re * dt)
    lam_re = mag * jnp.cos(a_im * dt)
    lam_im = mag * jnp.sin(a_im * dt)
    den = a_re * a_re + a_im * a_im
    num_re = lam_re - 1.0
    coef_re = (num_re * a_re + lam_im * a_im) / den
    coef_im = (lam_im * a_re - num_re * a_im) / den
    bb_re, bb_im = cmul(coef_re[..., None], coef_im[..., None], b_re.astype(f32), b_im.astype(f32))
    c_re = c_re.astype(f32)
    c_im = c_im.astype(f32)
    n_chunks = seq // S5_CHUNK
    u_chunks = u.reshape(bsz, n_chunks, S5_CHUNK, S5_GROUPS, S5_GROUP).transpose(1, 0, 2, 3, 4)
    lam_re_b = jnp.broadcast_to(lam_re, (bsz, S5_CHUNK, S5_GROUPS, S5_STATE))
    lam_im_b = jnp.broadcast_to(lam_im, (bsz, S5_CHUNK, S5_GROUPS, S5_STATE))

    def combine(left, right):
        a1r, a1i, b1r, b1i = left
        a2r, a2i, b2r, b2i = right
        ar, ai = cmul(a2r, a2i, a1r, a1i)
        br, bi = cmul(a2r, a2i, b1r, b1i)
        return ar, ai, br + b2r, bi + b2i

    def step(carry, u_c):
        h_re0, h_im0 = carry
        bu_re = jnp.einsum('bcgn,gpn->bcgp', u_c, bb_re)
        bu_im = jnp.einsum('bcgn,gpn->bcgp', u_c, bb_im)
        pr, pi, sr, si = lax.associative_scan(combine, (lam_re_b, lam_im_b, bu_re, bu_im), axis=1)
        cr, ci = cmul(pr, pi, h_re0[:, None], h_im0[:, None])
        st_re = sr + cr
        st_im = si + ci
        y = jnp.einsum('bcgp,gnp->bcgn', st_re, c_re) - jnp.einsum('bcgp,gnp->bcgn', st_im, c_im)
        return (st_re[:, -1], st_im[:, -1]), y

    init = (jnp.zeros((bsz, S5_GROUPS, S5_STATE), f32), jnp.zeros((bsz, S5_GROUPS, S5_STATE), f32))
    _, ys = lax.scan(step, init, u_chunks)
    y = ys.transpose(1, 0, 2, 3, 4).reshape(bsz, seq, S5_WIDTH) + d_skip.astype(f32) * u
    g = jax.nn.gelu(y)
    g = g * jax.nn.sigmoid(g @ w_glu.astype(f32))
    return g.astype(h.dtype) @ w_out


def partial_rope(x, cos, sin):
    half = ROT_DIM // 2
    x1 = x[..., :half]
    x2 = x[..., half:ROT_DIM]
    c = cos[None, :, None, :]
    s = sin[None, :, None, :]
    return jnp.concatenate([x1 * c - x2 * s, x2 * c + x1 * s, x[..., ROT_DIM:]], axis=-1)


def moba_mixer(h, w_qkv, q_gain, k_gain, w_out):
    f32 = jnp.float32
    bsz, seq, _ = h.shape
    qkv = (h @ w_qkv).reshape(bsz, seq, 3, N_HEADS, HEAD_DIM)
    q = rms_norm(qkv[:, :, 0], q_gain).astype(f32)
    k = rms_norm(qkv[:, :, 1], k_gain).astype(f32)
    v = qkv[:, :, 2].astype(f32)
    pos = jnp.arange(seq, dtype=f32)
    inv_freq = ROPE_THETA ** (-jnp.arange(0, ROT_DIM, 2, dtype=f32) / ROT_DIM)
    ang = pos[:, None] * inv_freq[None, :]
    cos, sin = jnp.cos(ang), jnp.sin(ang)
    q = partial_rope(q, cos, sin) * (HEAD_DIM ** -0.5)
    k = partial_rope(k, cos, sin)
    n_blocks = -(-seq // MOBA_BLOCK)
    pad = n_blocks * MOBA_BLOCK - seq
    k_pad = jnp.pad(k, ((0, 0), (0, pad), (0, 0), (0, 0)))
    v_pad = jnp.pad(v, ((0, 0), (0, pad), (0, 0), (0, 0)))
    k_blk = k_pad.reshape(bsz, n_blocks, MOBA_BLOCK, N_HEADS, HEAD_DIM)
    v_blk = v_pad.reshape(bsz, n_blocks, MOBA_BLOCK, N_HEADS, HEAD_DIM)
    k_mean = k_blk.mean(axis=2)
    k_bh = k_blk.transpose(0, 3, 1, 2, 4)
    v_bh = v_blk.transpose(0, 3, 1, 2, 4)
    top_k = min(MOBA_TOPK, n_blocks)
    b_idx = jnp.arange(bsz)[:, None, None]
    h_idx = jnp.arange(N_HEADS)[None, None, :]
    blk_ids = jnp.arange(n_blocks)

    def attend_chunk(c):
        start = c * Q_BLOCK
        q_c = lax.dynamic_slice_in_dim(q, start, Q_BLOCK, axis=1)
        q_pos = start + jnp.arange(Q_BLOCK)
        own = start // MOBA_BLOCK
        gate = jnp.einsum('bqhd,bnhd->bqhn', q_c, k_mean)
        gate = jnp.where(blk_ids < own, gate, -jnp.inf)
        _, sel = lax.top_k(gate, top_k)
        k_own = lax.dynamic_slice_in_dim(k_pad, own * MOBA_BLOCK, MOBA_BLOCK, axis=1)
        v_own = lax.dynamic_slice_in_dim(v_pad, own * MOBA_BLOCK, MOBA_BLOCK, axis=1)
        k_pos = own * MOBA_BLOCK + jnp.arange(MOBA_BLOCK)
        s_own = jnp.einsum('bqhd,bkhd->bqhk', q_c, k_own)
        causal = (k_pos[None, :] <= q_pos[:, None])[None, :, None, :]
        s_own = jnp.where(causal, s_own, -jnp.inf)
        scores = []
        for j in range(top_k):
            k_sel = k_bh[b_idx, h_idx, sel[..., j]]
            s_j = jnp.einsum('bqhd,bqhkd->bqhk', q_c, k_sel)
            scores.append(jnp.where(j < own, s_j, -jnp.inf))
        scores.append(s_own)
        p = jax.nn.softmax(jnp.concatenate(scores, axis=-1), axis=-1)
        p_parts = jnp.split(p, top_k + 1, axis=-1)
        out = jnp.einsum('bqhk,bkhd->bqhd', p_parts[-1], v_own)
        for j in range(top_k):
            v_sel = v_bh[b_idx, h_idx, sel[..., j]]
            out = out + jnp.einsum('bqhk,bqhkd->bqhd', p_parts[j], v_sel)
        return out

    outs = lax.map(attend_chunk, jnp.arange(seq // Q_BLOCK))
    o = outs.transpose(1, 0, 2, 3, 4).reshape(bsz, seq, ATTN_WIDTH).astype(h.dtype)
    return o @ w_out


def setup_inputs(seed: int = 0) -> dict:
    key = jax.random.key(seed)
    ks = jax.random.split(key, 24)
    f32 = jnp.float32
    nrm = lambda k, shape, scale: jax.random.normal(k, shape, f32) * scale
    x = nrm(ks[0], (BATCH, SEQ, D_MODEL), 1.0)
    ffn_norm = 1.0 + nrm(ks[1], (DEPTH, 2, D_MODEL), 0.02)
    ffn_w_gate = nrm(ks[2], (DEPTH, 2, D_MODEL, D_FF), D_MODEL ** -0.5)
    ffn_w_up = nrm(ks[3], (DEPTH, 2, D_MODEL, D_FF), D_MODEL ** -0.5)
    ffn_w_down = nrm(ks[4], (DEPTH, 2, D_FF, D_MODEL), D_FF ** -0.5)
    mix_norm = 1.0 + nrm(ks[5], (DEPTH, D_MODEL), 0.02)
    s5_w_in = nrm(ks[6], (N_A, D_MODEL, S5_WIDTH), D_MODEL ** -0.5)
    s5_a_re = -0.5 + nrm(ks[7], (N_A, S5_GROUPS, S5_STATE), 0.01)
    s5_a_im = math.pi * jnp.arange(S5_STATE, dtype=f32)[None, None, :] + nrm(ks[8], (N_A, S5_GROUPS, S5_STATE), 0.01)
    s5_log_dt = jax.random.uniform(ks[9], (N_A, S5_GROUPS), f32, math.log(S5_DT_MIN), math.log(S5_DT_MAX))
    s5_b_re = nrm(ks[10], (N_A, S5_GROUPS, S5_STATE, S5_GROUP), (2 * S5_GROUP) ** -0.5)
    s5_b_im = nrm(ks[11], (N_A, S5_GROUPS, S5_STATE, S5_GROUP), (2 * S5_GROUP) ** -0.5)
    s5_c_re = nrm(ks[12], (N_A, S5_GROUPS, S5_GROUP, S5_STATE), S5_STATE ** -0.5)
    s5_c_im = nrm(ks[13], (N_A, S5_GROUPS, S5_GROUP, S5_STATE), S5_STATE ** -0.5)
    s5_d = nrm(ks[14], (N_A, S5_WIDTH), 1.0)
    s5_w_glu = nrm(ks[15], (N_A, S5_WIDTH, S5_WIDTH), S5_WIDTH ** -0.5)
    s5_w_out = nrm(ks[16], (N_A, S5_WIDTH, D_MODEL), S5_WIDTH ** -0.5)
    moba_w_qkv = nrm(ks[17], (N_B, D_MODEL, 3 * ATTN_WIDTH), D_MODEL ** -0.5)
    moba_q_norm = 1.0 + nrm(ks[18], (N_B, HEAD_DIM), 0.02)
    moba_k_norm = 1.0 + nrm(ks[19], (N_B, HEAD_DIM), 0.02)
    moba_w_out = nrm(ks[20], (N_B, ATTN_WIDTH, D_MODEL), ATTN_WIDTH ** -0.5)
    return {'x': x, 'ffn_norm': ffn_norm, 'ffn_w_gate': ffn_w_gate, 'ffn_w_up': ffn_w_up,
            'ffn_w_down': ffn_w_down, 'mix_norm': mix_norm, 's5_w_in': s5_w_in,
            's5_a_re': s5_a_re, 's5_a_im': s5_a_im, 's5_log_dt': s5_log_dt,
            's5_b_re': s5_b_re, 's5_b_im': s5_b_im, 's5_c_re': s5_c_re, 's5_c_im': s5_c_im,
            's5_d': s5_d, 's5_w_glu': s5_w_glu, 's5_w_out': s5_w_out,
            'moba_w_qkv': moba_w_qkv, 'moba_q_norm': moba_q_norm, 'moba_k_norm': moba_k_norm,
            'moba_w_out': moba_w_out}


def reference(x, ffn_norm, ffn_w_gate, ffn_w_up, ffn_w_down, mix_norm, s5_w_in,
              s5_a_re, s5_a_im, s5_log_dt, s5_b_re, s5_b_im, s5_c_re, s5_c_im,
              s5_d, s5_w_glu, s5_w_out, moba_w_qkv, moba_q_norm, moba_k_norm, moba_w_out):
    h = x
    for layer in range(DEPTH):
        h = h + MACARON_WEIGHT * swiglu(rms_norm(h, ffn_norm[layer, 0]), ffn_w_gate[layer, 0],
                                        ffn_w_up[layer, 0], ffn_w_down[layer, 0])
        hn = rms_norm(h, mix_norm[layer])
        i = layer // N_MIXERS
        if layer % N_MIXERS == 0:
            h = h + s5_mixer(hn, s5_w_in[i], s5_a_re[i], s5_a_im[i], s5_log_dt[i], s5_b_re[i],
                             s5_b_im[i], s5_c_re[i], s5_c_im[i], s5_d[i], s5_w_glu[i], s5_w_out[i])
        else:
            h = h + moba_mixer(hn, moba_w_qkv[i], moba_q_norm[i], moba_k_norm[i], moba_w_out[i])
        h = h + MACARON_WEIGHT * swiglu(rms_norm(h, ffn_norm[layer, 1]), ffn_w_gate[layer, 1],
                                        ffn_w_up[layer, 1], ffn_w_down[layer, 1])
    return h
```

```python
import functools

import jax
import jax.numpy as jnp
from jax import lax
from jax.experimental import pallas as pl
from jax.experimental.pallas import tpu as pltpu

F32 = jnp.float32
BF16 = jnp.bfloat16

LANES = 128
SUBLANES = 8
VMEM_LIMIT_BYTES = 56 * 1024 * 1024

EPS = 1e-6
MACARON_WEIGHT = 0.5
S5_GROUP = 16
S5_STATE = 64
GROUPS_PER_SLAB = LANES // S5_GROUP
SLAB_STATES = GROUPS_PER_SLAB * S5_STATE
N_HEADS = 8
HEAD_DIM = 128
MOBA_BLOCK = 256
MOBA_TOPK = 3
Q_BLOCK = 128
ROPE_THETA = 500000.0
ROT_DIM = HEAD_DIM // 4
ROT_HALF = ROT_DIM // 2

FFN_ROWS = 512
FFN_CHUNK = 256
PROJ_ROWS = 512
SCAN_ROWS = 256


def _params(*semantics):
    return pltpu.CompilerParams(dimension_semantics=semantics,
                                vmem_limit_bytes=VMEM_LIMIT_BYTES)


def _rms_normed(x, gain):
    ms = jnp.mean(x * x, axis=-1, keepdims=True)
    return x * lax.rsqrt(ms + EPS) * gain


def _ffn_body(x_ref, g_ref, wg_ref, wu_ref, wd_ref, o_ref, act_ref):
    x = x_ref[...]
    xn = _rms_normed(x, g_ref[...]).astype(BF16)
    d_ff = wg_ref.shape[1]
    for j in range(d_ff // FFN_CHUNK):
        cols = slice(j * FFN_CHUNK, (j + 1) * FFN_CHUNK)
        gate = jnp.dot(xn, wg_ref[:, cols], preferred_element_type=F32)
        up = jnp.dot(xn, wu_ref[:, cols], preferred_element_type=F32)
        act_ref[:, cols] = (gate * jax.nn.sigmoid(gate) * up).astype(BF16)
    y = jnp.dot(act_ref[...], wd_ref[...], preferred_element_type=F32)
    o_ref[...] = x + MACARON_WEIGHT * y


def _ffn(h, gain, w_gate, w_up, w_down):
    seq, d = h.shape
    d_ff = w_gate.shape[1]
    return pl.pallas_call(
        _ffn_body,
        out_shape=jax.ShapeDtypeStruct((seq, d), F32),
        grid=(seq // FFN_ROWS,),
        in_specs=[
            pl.BlockSpec((FFN_ROWS, d), lambda i: (i, 0)),
            pl.BlockSpec((1, d), lambda i: (0, 0)),
            pl.BlockSpec((d, d_ff), lambda i: (0, 0)),
            pl.BlockSpec((d, d_ff), lambda i: (0, 0)),
            pl.BlockSpec((d_ff, d), lambda i: (0, 0)),
        ],
        out_specs=pl.BlockSpec((FFN_ROWS, d), lambda i: (i, 0)),
        scratch_shapes=[pltpu.VMEM((FFN_ROWS, d_ff), BF16)],
        compiler_params=_params("arbitrary"),
        name="ffn",
    )(h, gain.reshape(1, d), w_gate.astype(BF16), w_up.astype(BF16), w_down.astype(BF16))


def _norm_proj_body(x_ref, g_ref, w_ref, o_ref):
    xn = _rms_normed(x_ref[...], g_ref[...]).astype(BF16)
    o_ref[...] = jnp.dot(xn, w_ref[...], preferred_element_type=F32)


def _norm_proj(h, gain, w):
    seq, d = h.shape
    n = w.shape[1]
    return pl.pallas_call(
        _norm_proj_body,
        out_shape=jax.ShapeDtypeStruct((seq, n), F32),
        grid=(seq // PROJ_ROWS,),
        in_specs=[
            pl.BlockSpec((PROJ_ROWS, d), lambda i: (i, 0)),
            pl.BlockSpec((1, d), lambda i: (0, 0)),
            pl.BlockSpec((d, n), lambda i: (0, 0)),
        ],
        out_specs=pl.BlockSpec((PROJ_ROWS, n), lambda i: (i, 0)),
        compiler_params=_params("arbitrary"),
        name="norm_proj",
    )(h, gain.reshape(1, d), w.astype(BF16))


def _proj_res_body(h_ref, a_ref, w_ref, o_ref):
    o_ref[...] = h_ref[...] + jnp.dot(a_ref[...], w_ref[...], preferred_element_type=F32)


def _proj_res(h, a, w):
    seq, d = h.shape
    k = a.shape[1]
    return pl.pallas_call(
        _proj_res_body,
        out_shape=jax.ShapeDtypeStruct((seq, d), F32),
        grid=(seq // PROJ_ROWS,),
        in_specs=[
            pl.BlockSpec((PROJ_ROWS, d), lambda i: (i, 0)),
            pl.BlockSpec((PROJ_ROWS, k), lambda i: (i, 0)),
            pl.BlockSpec((k, d), lambda i: (0, 0)),
        ],
        out_specs=pl.BlockSpec((PROJ_ROWS, d), lambda i: (i, 0)),
        compiler_params=_params("arbitrary"),
        name="proj_res",
    )(h, a, w.astype(BF16))


def _s5_prep_body(are_ref, aim_ref, ldt_ref, bre_ref, bim_ref,
                  pre_ref, pim_ref, bbre_ref, bbim_ref):
    a_re = are_ref[...]
    a_im = aim_ref[...]
    dt = jnp.exp(ldt_ref[...])
    mag = jnp.exp(a_re * dt)
    lam_re = mag * jnp.cos(a_im * dt)
    lam_im = mag * jnp.sin(a_im * dt)
    den = a_re * a_re + a_im * a_im
    num_re = lam_re - 1.0
    coef_re = (num_re * a_re + lam_im * a_im) / den
    coef_im = (lam_im * a_re - num_re * a_im) / den
    b_re = bre_ref[...]
    b_im = bim_ref[...]
    bbre_ref[...] = coef_re * b_re - coef_im * b_im
    bbim_ref[...] = coef_re * b_im + coef_im * b_re
    p_re, p_im = lam_re, lam_im
    for r in range(SUBLANES):
        pre_ref[r:r + 1, :] = p_re
        pim_ref[r:r + 1, :] = p_im
        p_re, p_im = p_re * lam_re - p_im * lam_im, p_re * lam_im + p_im * lam_re


def _s5_prep(a_re, a_im, log_dt, b_re, b_im):
    groups, states = a_re.shape
    n_flat = groups * states
    row = lambda a: a.astype(F32).reshape(1, n_flat)
    ldt = jnp.broadcast_to(log_dt.astype(F32)[:, None], (groups, states)).reshape(1, n_flat)
    chan_rows = lambda b: b.astype(F32).transpose(2, 0, 1).reshape(S5_GROUP, n_flat)
    return pl.pallas_call(
        _s5_prep_body,
        out_shape=(
            jax.ShapeDtypeStruct((SUBLANES, n_flat), F32),
            jax.ShapeDtypeStruct((SUBLANES, n_flat), F32),
            jax.ShapeDtypeStruct((S5_GROUP, n_flat), F32),
            jax.ShapeDtypeStruct((S5_GROUP, n_flat), F32),
        ),
        name="s5_prep",
    )(row(a_re), row(a_im), ldt, chan_rows(b_re), chan_rows(b_im))


def _s5_scan_body(u_ref, wb_ref, wcre_ref, wcim_ref, pre_ref, pim_ref, y_ref,
                  bu_ref, st_ref, carry_ref):
    t_idx = pl.program_id(1)
    ns = SLAB_STATES

    @pl.when(t_idx == 0)
    def _():
        carry_ref[...] = jnp.zeros_like(carry_ref)

    bu_ref[...] = jnp.dot(u_ref[...].astype(BF16), wb_ref[0], preferred_element_type=F32)

    p_re = pre_ref[0]
    p_im = pim_ref[0]
    row = lax.broadcasted_iota(jnp.int32, (SUBLANES, ns), 0)
    steps = []
    for s in (1, 2, 4):
        keep = row >= s
        steps.append((s,
                      jnp.where(keep, p_re[s - 1:s, :], 0.0),
                      jnp.where(keep, p_im[s - 1:s, :], 0.0)))

    def tile_step(i, carry):
        c_re, c_im = carry
        r0 = pl.multiple_of(i * SUBLANES, SUBLANES)
        x_re = bu_ref[pl.ds(r0, SUBLANES), 0:ns]
        x_im = bu_ref[pl.ds(r0, SUBLANES), ns:2 * ns]
        for s, a_re, a_im in steps:
            s_re = pltpu.roll(x_re, s, 0)
            s_im = pltpu.roll(x_im, s, 0)
            x_re, x_im = (x_re + (a_re * s_re - a_im * s_im),
                          x_im + (a_re * s_im + a_im * s_re))
        x_re, x_im = (x_re + (p_re * c_re - p_im * c_im),
                      x_im + (p_re * c_im + p_im * c_re))
        st_ref[pl.ds(r0, SUBLANES), 0:ns] = x_re
        st_ref[pl.ds(r0, SUBLANES), ns:2 * ns] = x_im
        last = SUBLANES - 1
        return (jnp.broadcast_to(x_re[last:last + 1, :], (SUBLANES, ns)),
                jnp.broadcast_to(x_im[last:last + 1, :], (SUBLANES, ns)))

    n_tiles = u_ref.shape[0] // SUBLANES
    c_re, c_im = lax.fori_loop(0, n_tiles, tile_step,
                               (carry_ref[:, 0:ns], carry_ref[:, ns:2 * ns]), unroll=2)
    carry_ref[:, 0:ns] = c_re
    carry_ref[:, ns:2 * ns] = c_im

    y_ref[...] = (
        jnp.dot(st_ref[:, 0:ns].astype(BF16), wcre_ref[0], preferred_element_type=F32)
        - jnp.dot(st_ref[:, ns:2 * ns].astype(BF16), wcim_ref[0], preferred_element_type=F32))


def _s5_scan(u, wb, wc_re, wc_im, pow_re, pow_im):
    seq, width = u.shape
    n_slabs = width // LANES
    ns = SLAB_STATES
    return pl.pallas_call(
        _s5_scan_body,
        out_shape=jax.ShapeDtypeStruct((seq, width), F32),
        grid=(n_slabs, seq // SCAN_ROWS),
        in_specs=[
            pl.BlockSpec((SCAN_ROWS, LANES), lambda b, t: (t, b)),
            pl.BlockSpec((1, LANES, 2 * ns), lambda b, t: (b, 0, 0)),
            pl.BlockSpec((1, ns, LANES), lambda b, t: (b, 0, 0)),
            pl.BlockSpec((1, ns, LANES), lambda b, t: (b, 0, 0)),
            pl.BlockSpec((1, SUBLANES, ns), lambda b, t: (b, 0, 0)),
            pl.BlockSpec((1, SUBLANES, ns), lambda b, t: (b, 0, 0)),
        ],
        out_specs=pl.BlockSpec((SCAN_ROWS, LANES), lambda b, t: (t, b)),
        scratch_shapes=[
            pltpu.VMEM((SCAN_ROWS, 2 * ns), F32),
            pltpu.VMEM((SCAN_ROWS, 2 * ns), F32),
            pltpu.VMEM((SUBLANES, 2 * ns), F32),
        ],
        compiler_params=_params("arbitrary", "arbitrary"),
        name="s5_scan",
    )(u, wb, wc_re, wc_im, pow_re, pow_im)


def _slab_block_diag(w):
    groups, a, b = w.shape
    n_slabs = groups // GROUPS_PER_SLAB
    eye = jnp.eye(GROUPS_PER_SLAB, dtype=w.dtype)
    w4 = w.reshape(n_slabs, GROUPS_PER_SLAB, a, b)
    out = jnp.einsum('jgab,gh->jgahb', w4, eye)
    return out.reshape(n_slabs, GROUPS_PER_SLAB * a, GROUPS_PER_SLAB * b)


def _s5_out_body(h_ref, y_ref, u_ref, d_ref, wglu_ref, wout_ref, o_ref):
    g = jax.nn.gelu(y_ref[...] + d_ref[...] * u_ref[...])
    gate = jnp.dot(g.astype(BF16), wglu_ref[...], preferred_element_type=F32)
    g = g * jax.nn.sigmoid(gate)
    o_ref[...] = h_ref[...] + jnp.dot(g.astype(BF16), wout_ref[...], preferred_element_type=F32)


def _s5_out(h, y, u, d_skip, w_glu, w_out):
    seq, d = h.shape
    width = u.shape[1]
    row_spec = lambda n: pl.BlockSpec((PROJ_ROWS, n), lambda i: (i, 0))
    return pl.pallas_call(
        _s5_out_body,
        out_shape=jax.ShapeDtypeStruct((seq, d), F32),
        grid=(seq // PROJ_ROWS,),
        in_specs=[
            row_spec(d), row_spec(width), row_spec(width),
            pl.BlockSpec((1, width), lambda i: (0, 0)),
            pl.BlockSpec((width, width), lambda i: (0, 0)),
            pl.BlockSpec((width, d), lambda i: (0, 0)),
        ],
        out_specs=row_spec(d),
        compiler_params=_params("arbitrary"),
        name="s5_out",
    )(h, y, u, d_skip.astype(F32).reshape(1, width), w_glu.astype(BF16), w_out.astype(BF16))


def _s5_mixer(h, gain, w_in, a_re, a_im, log_dt, b_re, b_im, c_re, c_im, d_skip, w_glu, w_out):
    groups, states = a_re.shape
    u = _norm_proj(h, gain, w_in)
    pow_re, pow_im, bb_re, bb_im = _s5_prep(a_re, a_im, log_dt, b_re, b_im)
    n_slabs = groups // GROUPS_PER_SLAB
    to_slabs = lambda p: p.reshape(SUBLANES, n_slabs, SLAB_STATES).transpose(1, 0, 2)
    per_group_b = lambda b: b.reshape(S5_GROUP, groups, states).transpose(1, 0, 2)
    wb = jnp.concatenate([_slab_block_diag(per_group_b(bb_re)),
                          _slab_block_diag(per_group_b(bb_im))], axis=-1).astype(BF16)
    per_group_c = lambda c: c.astype(F32).transpose(0, 2, 1)
    wc_re = _slab_block_diag(per_group_c(c_re)).astype(BF16)
    wc_im = _slab_block_diag(per_group_c(c_im)).astype(BF16)
    y = _s5_scan(u, wb, wc_re, wc_im, to_slabs(pow_re), to_slabs(pow_im))
    return _s5_out(h, y, u, d_skip, w_glu, w_out)


def _rope(x, cos, sin_lo, sin_hi):
    return (x * cos + pltpu.roll(x, HEAD_DIM - ROT_HALF, 1) * sin_lo
            + pltpu.roll(x, ROT_HALF, 1) * sin_hi)


def _qkv_body(x_ref, g_ref, w_ref, qg_ref, kg_ref, invf_ref, q_ref, k_ref, v_ref):
    rows = x_ref.shape[0]
    width = q_ref.shape[1]
    xn = _rms_normed(x_ref[...], g_ref[...]).astype(BF16)

    pos = (pl.program_id(0) * rows
           + lax.broadcasted_iota(jnp.int32, (rows, HEAD_DIM), 0)).astype(F32)
    ang = pos * invf_ref[...]
    lane = lax.broadcasted_iota(jnp.int32, (rows, HEAD_DIM), 1)
    cos = jnp.cos(ang)
    sin = jnp.sin(ang)
    sin_lo = jnp.where(lane < ROT_HALF, -sin, 0.0)
    sin_hi = jnp.where((lane >= ROT_HALF) & (lane < ROT_DIM), sin, 0.0)

    def heads(part, gain, scale, out_ref):
        y = jnp.dot(xn, w_ref[:, part * width:(part + 1) * width], preferred_element_type=F32)
        for hd in range(width // HEAD_DIM):
            cols = slice(hd * HEAD_DIM, (hd + 1) * HEAD_DIM)
            yh = y[:, cols]
            if gain is not None:
                yh = _rope(_rms_normed(yh, gain), cos, sin_lo, sin_hi)
                if scale is not None:
                    yh = yh * scale
            out_ref[:, cols] = yh.astype(out_ref.dtype)

    heads(0, qg_ref[...], HEAD_DIM ** -0.5, q_ref)
    heads(1, kg_ref[...], None, k_ref)
    heads(2, None, None, v_ref)


def _qkv(h, gain, w_qkv, q_gain, k_gain):
    seq, d = h.shape
    width = w_qkv.shape[1] // 3
    inv_freq = ROPE_THETA ** (-jnp.arange(0, ROT_DIM, 2, dtype=F32) / ROT_DIM)
    invf = jnp.concatenate([inv_freq, inv_freq,
                            jnp.zeros((HEAD_DIM - ROT_DIM,), F32)]).reshape(1, HEAD_DIM)
    out = jax.ShapeDtypeStruct((seq, width), BF16)
    row_spec = lambda n: pl.BlockSpec((PROJ_ROWS, n), lambda i: (i, 0))
    const = lambda shape: pl.BlockSpec(shape, lambda i: (0, 0))
    return pl.pallas_call(
        _qkv_body,
        out_shape=(out, out, out),
        grid=(seq // PROJ_ROWS,),
        in_specs=[row_spec(d), const((1, d)), const((d, 3 * width)),
                  const((1, HEAD_DIM)), const((1, HEAD_DIM)), const((1, HEAD_DIM))],
        out_specs=(row_spec(width), row_spec(width), row_spec(width)),
        compiler_params=_params("arbitrary"),
        name="moba_qkv",
    )(h, gain.reshape(1, d), w_qkv.astype(BF16),
      q_gain.astype(F32).reshape(1, HEAD_DIM), k_gain.astype(F32).reshape(1, HEAD_DIM), invf)


def _attn_body(q_ref, k_ref, v_ref, o_ref, kmean_ref):
    chunk = pl.program_id(1)
    n_blocks = k_ref.shape[0] // MOBA_BLOCK
    own = chunk // (MOBA_BLOCK // Q_BLOCK)
    neg_inf = -jnp.inf

    @pl.when(chunk == 0)
    def _():
        for b in range(n_blocks):
            kb = k_ref[b * MOBA_BLOCK:(b + 1) * MOBA_BLOCK, :].astype(F32)
            kmean_ref[b:b + 1, :] = jnp.mean(kb, axis=0, keepdims=True)

    q = q_ref[...]
    nt = (((1,), (1,)), ((), ()))

    gate = lax.dot_general(q, kmean_ref[...].astype(BF16), nt, preferred_element_type=F32)
    blk = lax.broadcasted_iota(jnp.int32, gate.shape, 1).astype(F32)
    own_f = own.astype(F32)
    gate = jnp.where(blk < own_f, gate, neg_inf)
    selected = jnp.zeros(gate.shape, F32)
    for j in range(min(MOBA_TOPK, n_blocks)):
        top = jnp.max(gate, axis=1, keepdims=True)
        first = jnp.min(jnp.where(gate == top, blk, float(n_blocks)), axis=1, keepdims=True)
        hit = blk == first
        selected = jnp.maximum(selected, jnp.where(hit, jnp.where(own > j, 1.0, 0.0), 0.0))
        gate = jnp.where(hit, neg_inf, gate)

    k0 = pl.multiple_of(own * MOBA_BLOCK, MOBA_BLOCK)
    s = lax.dot_general(q, k_ref[pl.ds(k0, MOBA_BLOCK), :], nt, preferred_element_type=F32)
    q_pos = chunk * Q_BLOCK + lax.broadcasted_iota(jnp.int32, s.shape, 0)
    k_pos = own * MOBA_BLOCK + lax.broadcasted_iota(jnp.int32, s.shape, 1)
    s = jnp.where(k_pos <= q_pos, s, neg_inf)
    m = jnp.max(s, axis=1, keepdims=True)
    p = jnp.exp(s - m)
    l = jnp.sum(p, axis=1, keepdims=True)
    acc = jnp.dot(p.astype(BF16), v_ref[pl.ds(k0, MOBA_BLOCK), :], preferred_element_type=F32)

    def past_block(b, carry):
        m, l, acc = carry
        kb0 = pl.multiple_of(b * MOBA_BLOCK, MOBA_BLOCK)
        s = lax.dot_general(q, k_ref[pl.ds(kb0, MOBA_BLOCK), :], nt, preferred_element_type=F32)
        chosen = jnp.sum(jnp.where(blk == b.astype(F32), selected, 0.0), axis=1, keepdims=True)
        s = s + jnp.where(chosen > 0.0, 0.0, neg_inf)
        m_new = jnp.maximum(m, jnp.max(s, axis=1, keepdims=True))
        alpha = jnp.exp(m - m_new)
        p = jnp.exp(s - m_new)
        l = alpha * l + jnp.sum(p, axis=1, keepdims=True)
        acc = alpha * acc + jnp.dot(p.astype(BF16), v_ref[pl.ds(kb0, MOBA_BLOCK), :],
                                    preferred_element_type=F32)
        return m_new, l, acc

    m, l, acc = lax.fori_loop(0, own, past_block, (m, l, acc))
    o_ref[...] = (acc / l).astype(o_ref.dtype)


def _attention(q, k, v):
    seq, width = q.shape
    n_heads = width // HEAD_DIM
    n_blocks = seq // MOBA_BLOCK
    head_spec = pl.BlockSpec((seq, HEAD_DIM), lambda hd, c: (0, hd))
    chunk_spec = pl.BlockSpec((Q_BLOCK, HEAD_DIM), lambda hd, c: (c, hd))
    return pl.pallas_call(
        _attn_body,
        out_shape=jax.ShapeDtypeStruct((seq, width), BF16),
        grid=(n_heads, seq // Q_BLOCK),
        in_specs=[chunk_spec, head_spec, head_spec],
        out_specs=chunk_spec,
        scratch_shapes=[pltpu.VMEM((n_blocks, HEAD_DIM), F32)],
        compiler_params=_params("arbitrary", "arbitrary"),
        name="moba_attn",
    )(q, k, v)


def _moba_mixer(h, gain, w_qkv, q_gain, k_gain, w_out):
    q, k, v = _qkv(h, gain, w_qkv, q_gain, k_gain)
    o = _attention(q, k, v)
    return _proj_res(h, o, w_out)


def kernel(x, ffn_norm, ffn_w_gate, ffn_w_up, ffn_w_down, mix_norm, s5_w_in, s5_a_re, s5_a_im, s5_log_dt, s5_b_re, s5_b_im, s5_c_re, s5_c_im, s5_d, s5_w_glu, s5_w_out, moba_w_qkv, moba_q_norm, moba_k_norm, moba_w_out):
    bsz, seq, d = x.shape
    depth = ffn_norm.shape[0]
    outs = []
    for b in range(bsz):
        h = x[b].astype(F32)
        for layer in range(depth):
            h = _ffn(h, ffn_norm[layer, 0], ffn_w_gate[layer, 0], ffn_w_up[layer, 0],
                     ffn_w_down[layer, 0])
            i = layer // 2
            if layer % 2 == 0:
                h = _s5_mixer(h, mix_norm[layer], s5_w_in[i], s5_a_re[i], s5_a_im[i],
                              s5_log_dt[i], s5_b_re[i], s5_b_im[i], s5_c_re[i], s5_c_im[i],
                              s5_d[i], s5_w_glu[i], s5_w_out[i])
            else:
                h = _moba_mixer(h, mix_norm[layer], moba_w_qkv[i], moba_q_norm[i],
                                moba_k_norm[i], moba_w_out[i])
            h = _ffn(h, ffn_norm[layer, 1], ffn_w_gate[layer, 1], ffn_w_up[layer, 1],
                     ffn_w_down[layer, 1])
        outs.append(h)
    return jnp.stack(outs).astype(x.dtype)
```

```python
import functools

import jax
import jax.numpy as jnp
from jax import lax
from jax.experimental import pallas as pl
from jax.experimental.pallas import tpu as pltpu

F32 = jnp.float32
BF16 = jnp.bfloat16

LANES = 128
SUBLANES = 8
VMEM_LIMIT_BYTES = 56 * 1024 * 1024

EPS = 1e-6
MACARON_WEIGHT = 0.5
S5_GROUP = 16
S5_STATE = 64
GROUPS_PER_SLAB = LANES // S5_GROUP
SLAB_STATES = GROUPS_PER_SLAB * S5_STATE
N_HEADS = 8
HEAD_DIM = 128
MOBA_BLOCK = 256
MOBA_TOPK = 3
ROPE_THETA = 500000.0
ROT_DIM = HEAD_DIM // 4
ROT_HALF = ROT_DIM // 2

FFN_ROWS = 512
FFN_CHUNK = 256
PROJ_ROWS = 512
SCAN_ROWS = 256
ATTN_GROUP = 4
LOG2_E = 1.4426950408889634


def _params(*semantics):
    return pltpu.CompilerParams(dimension_semantics=semantics,
                                vmem_limit_bytes=VMEM_LIMIT_BYTES)


def _rms_normed(x, gain):
    ms = jnp.mean(x * x, axis=-1, keepdims=True)
    return x * lax.rsqrt(ms + EPS) * gain


def _ffn_body(x_ref, g_ref, wg_ref, wu_ref, wd_ref, o_ref, act_ref):
    x = x_ref[...]
    xn = _rms_normed(x, g_ref[...]).astype(BF16)
    d_ff = wg_ref.shape[1]
    for j in range(d_ff // FFN_CHUNK):
        cols = slice(j * FFN_CHUNK, (j + 1) * FFN_CHUNK)
        gate = jnp.dot(xn, wg_ref[:, cols], preferred_element_type=F32)
        up = jnp.dot(xn, wu_ref[:, cols], preferred_element_type=F32)
        act_ref[:, cols] = (gate * jax.nn.sigmoid(gate) * up).astype(BF16)
    y = jnp.dot(act_ref[...], wd_ref[...], preferred_element_type=F32)
    o_ref[...] = x + MACARON_WEIGHT * y


def _ffn(h, gain, w_gate, w_up, w_down):
    seq, d = h.shape
    d_ff = w_gate.shape[1]
    return pl.pallas_call(
        _ffn_body,
        out_shape=jax.ShapeDtypeStruct((seq, d), F32),
        grid=(seq // FFN_ROWS,),
        in_specs=[
            pl.BlockSpec((FFN_ROWS, d), lambda i: (i, 0)),
            pl.BlockSpec((1, d), lambda i: (0, 0)),
            pl.BlockSpec((d, d_ff), lambda i: (0, 0)),
            pl.BlockSpec((d, d_ff), lambda i: (0, 0)),
            pl.BlockSpec((d_ff, d), lambda i: (0, 0)),
        ],
        out_specs=pl.BlockSpec((FFN_ROWS, d), lambda i: (i, 0)),
        scratch_shapes=[pltpu.VMEM((FFN_ROWS, d_ff), BF16)],
        compiler_params=_params("arbitrary"),
        name="ffn",
    )(h, gain.reshape(1, d), w_gate.astype(BF16), w_up.astype(BF16), w_down.astype(BF16))


def _norm_proj_body(x_ref, g_ref, w_ref, o_ref):
    xn = _rms_normed(x_ref[...], g_ref[...]).astype(BF16)
    o_ref[...] = jnp.dot(xn, w_ref[...], preferred_element_type=F32)


def _norm_proj(h, gain, w):
    seq, d = h.shape
    n = w.shape[1]
    return pl.pallas_call(
        _norm_proj_body,
        out_shape=jax.ShapeDtypeStruct((seq, n), F32),
        grid=(seq // PROJ_ROWS,),
        in_specs=[
            pl.BlockSpec((PROJ_ROWS, d), lambda i: (i, 0)),
            pl.BlockSpec((1, d), lambda i: (0, 0)),
            pl.BlockSpec((d, n), lambda i: (0, 0)),
        ],
        out_specs=pl.BlockSpec((PROJ_ROWS, n), lambda i: (i, 0)),
        compiler_params=_params("arbitrary"),
        name="norm_proj",
    )(h, gain.reshape(1, d), w.astype(BF16))


def _proj_res_body(h_ref, a_ref, w_ref, o_ref):
    o_ref[...] = h_ref[...] + jnp.dot(a_ref[...], w_ref[...], preferred_element_type=F32)


def _proj_res(h, a, w):
    seq, d = h.shape
    k = a.shape[1]
    return pl.pallas_call(
        _proj_res_body,
        out_shape=jax.ShapeDtypeStruct((seq, d), F32),
        grid=(seq // PROJ_ROWS,),
        in_specs=[
            pl.BlockSpec((PROJ_ROWS, d), lambda i: (i, 0)),
            pl.BlockSpec((PROJ_ROWS, k), lambda i: (i, 0)),
            pl.BlockSpec((k, d), lambda i: (0, 0)),
        ],
        out_specs=pl.BlockSpec((PROJ_ROWS, d), lambda i: (i, 0)),
        compiler_params=_params("arbitrary"),
        name="proj_res",
    )(h, a, w.astype(BF16))


def _s5_prep_body(are_ref, aim_ref, ldt_ref, bre_ref, bim_ref,
                  pre_ref, pim_ref, bbre_ref, bbim_ref):
    a_re = are_ref[...]
    a_im = aim_ref[...]
    dt = jnp.exp(ldt_ref[...])
    mag = jnp.exp(a_re * dt)
    lam_re = mag * jnp.cos(a_im * dt)
    lam_im = mag * jnp.sin(a_im * dt)
    den = a_re * a_re + a_im * a_im
    num_re = lam_re - 1.0
    coef_re = (num_re * a_re + lam_im * a_im) / den
    coef_im = (lam_im * a_re - num_re * a_im) / den
    b_re = bre_ref[...]
    b_im = bim_ref[...]
    bbre_ref[...] = coef_re * b_re - coef_im * b_im
    bbim_ref[...] = coef_re * b_im + coef_im * b_re
    p_re, p_im = lam_re, lam_im
    for r in range(SUBLANES):
        pre_ref[r:r + 1, :] = p_re
        pim_ref[r:r + 1, :] = p_im
        p_re, p_im = p_re * lam_re - p_im * lam_im, p_re * lam_im + p_im * lam_re


def _s5_prep(a_re, a_im, log_dt, b_re, b_im):
    groups, states = a_re.shape
    n_flat = groups * states
    row = lambda a: a.astype(F32).reshape(1, n_flat)
    ldt = jnp.broadcast_to(log_dt.astype(F32)[:, None], (groups, states)).reshape(1, n_flat)
    chan_rows = lambda b: b.astype(F32).transpose(2, 0, 1).reshape(S5_GROUP, n_flat)
    return pl.pallas_call(
        _s5_prep_body,
        out_shape=(
            jax.ShapeDtypeStruct((SUBLANES, n_flat), F32),
            jax.ShapeDtypeStruct((SUBLANES, n_flat), F32),
            jax.ShapeDtypeStruct((S5_GROUP, n_flat), F32),
            jax.ShapeDtypeStruct((S5_GROUP, n_flat), F32),
        ),
        name="s5_prep",
    )(row(a_re), row(a_im), ldt, chan_rows(b_re), chan_rows(b_im))


def _s5_scan_body(u_ref, wb_ref, wcre_ref, wcim_ref, pre_ref, pim_ref, y_ref,
                  bu_ref, st_ref, carry_ref):
    t_idx = pl.program_id(1)
    ns = SLAB_STATES

    @pl.when(t_idx == 0)
    def _():
        carry_ref[...] = jnp.zeros_like(carry_ref)

    bu_ref[...] = jnp.dot(u_ref[...].astype(BF16), wb_ref[0], preferred_element_type=F32)

    p_re = pre_ref[0]
    p_im = pim_ref[0]
    row = lax.broadcasted_iota(jnp.int32, (SUBLANES, ns), 0)
    steps = []
    for s in (1, 2, 4):
        keep = row >= s
        steps.append((s,
                      jnp.where(keep, p_re[s - 1:s, :], 0.0),
                      jnp.where(keep, p_im[s - 1:s, :], 0.0)))

    def tile_step(i, carry):
        c_re, c_im = carry
        r0 = pl.multiple_of(i * SUBLANES, SUBLANES)
        x_re = bu_ref[pl.ds(r0, SUBLANES), 0:ns]
        x_im = bu_ref[pl.ds(r0, SUBLANES), ns:2 * ns]
        for s, a_re, a_im in steps:
            s_re = pltpu.roll(x_re, s, 0)
            s_im = pltpu.roll(x_im, s, 0)
            x_re, x_im = (x_re + (a_re * s_re - a_im * s_im),
                          x_im + (a_re * s_im + a_im * s_re))
        x_re, x_im = (x_re + (p_re * c_re - p_im * c_im),
                      x_im + (p_re * c_im + p_im * c_re))
        st_ref[pl.ds(r0, SUBLANES), 0:ns] = x_re
        st_ref[pl.ds(r0, SUBLANES), ns:2 * ns] = x_im
        last = SUBLANES - 1
        return (jnp.broadcast_to(x_re[last:last + 1, :], (SUBLANES, ns)),
                jnp.broadcast_to(x_im[last:last + 1, :], (SUBLANES, ns)))

    n_tiles = u_ref.shape[0] // SUBLANES
    c_re, c_im = lax.fori_loop(0, n_tiles, tile_step,
                               (carry_ref[:, 0:ns], carry_ref[:, ns:2 * ns]), unroll=2)
    carry_ref[:, 0:ns] = c_re
    carry_ref[:, ns:2 * ns] = c_im

    y_ref[...] = (
        jnp.dot(st_ref[:, 0:ns].astype(BF16), wcre_ref[0], preferred_element_type=F32)
        - jnp.dot(st_ref[:, ns:2 * ns].astype(BF16), wcim_ref[0], preferred_element_type=F32))


def _s5_scan(u, wb, wc_re, wc_im, pow_re, pow_im):
    seq, width = u.shape
    n_slabs = width // LANES
    ns = SLAB_STATES
    return pl.pallas_call(
        _s5_scan_body,
        out_shape=jax.ShapeDtypeStruct((seq, width), F32),
        grid=(n_slabs, seq // SCAN_ROWS),
        in_specs=[
            pl.BlockSpec((SCAN_ROWS, LANES), lambda b, t: (t, b)),
            pl.BlockSpec((1, LANES, 2 * ns), lambda b, t: (b, 0, 0)),
            pl.BlockSpec((1, ns, LANES), lambda b, t: (b, 0, 0)),
            pl.BlockSpec((1, ns, LANES), lambda b, t: (b, 0, 0)),
            pl.BlockSpec((1, SUBLANES, ns), lambda b, t: (b, 0, 0)),
            pl.BlockSpec((1, SUBLANES, ns), lambda b, t: (b, 0, 0)),
        ],
        out_specs=pl.BlockSpec((SCAN_ROWS, LANES), lambda b, t: (t, b)),
        scratch_shapes=[
            pltpu.VMEM((SCAN_ROWS, 2 * ns), F32),
            pltpu.VMEM((SCAN_ROWS, 2 * ns), F32),
            pltpu.VMEM((SUBLANES, 2 * ns), F32),
        ],
        compiler_params=_params("arbitrary", "arbitrary"),
        name="s5_scan",
    )(u, wb, wc_re, wc_im, pow_re, pow_im)


def _slab_block_diag(w):
    groups, a, b = w.shape
    n_slabs = groups // GROUPS_PER_SLAB
    eye = jnp.eye(GROUPS_PER_SLAB, dtype=w.dtype)
    w4 = w.reshape(n_slabs, GROUPS_PER_SLAB, a, b)
    out = jnp.einsum('jgab,gh->jgahb', w4, eye)
    return out.reshape(n_slabs, GROUPS_PER_SLAB * a, GROUPS_PER_SLAB * b)


def _s5_out_body(h_ref, y_ref, u_ref, d_ref, wglu_ref, wout_ref, o_ref):
    g = jax.nn.gelu(y_ref[...] + d_ref[...] * u_ref[...])
    gate = jnp.dot(g.astype(BF16), wglu_ref[...], preferred_element_type=F32)
    g = g * jax.nn.sigmoid(gate)
    o_ref[...] = h_ref[...] + jnp.dot(g.astype(BF16), wout_ref[...], preferred_element_type=F32)


def _s5_out(h, y, u, d_skip, w_glu, w_out):
    seq, d = h.shape
    width = u.shape[1]
    row_spec = lambda n: pl.BlockSpec((PROJ_ROWS, n), lambda i: (i, 0))
    return pl.pallas_call(
        _s5_out_body,
        out_shape=jax.ShapeDtypeStruct((seq, d), F32),
        grid=(seq // PROJ_ROWS,),
        in_specs=[
            row_spec(d), row_spec(width), row_spec(width),
            pl.BlockSpec((1, width), lambda i: (0, 0)),
            pl.BlockSpec((width, width), lambda i: (0, 0)),
            pl.BlockSpec((width, d), lambda i: (0, 0)),
        ],
        out_specs=row_spec(d),
        compiler_params=_params("arbitrary"),
        name="s5_out",
    )(h, y, u, d_skip.astype(F32).reshape(1, width), w_glu.astype(BF16), w_out.astype(BF16))


def _s5_mixer(h, gain, w_in, a_re, a_im, log_dt, b_re, b_im, c_re, c_im, d_skip, w_glu, w_out):
    groups, states = a_re.shape
    u = _norm_proj(h, gain, w_in)
    pow_re, pow_im, bb_re, bb_im = _s5_prep(a_re, a_im, log_dt, b_re, b_im)
    n_slabs = groups // GROUPS_PER_SLAB
    to_slabs = lambda p: p.reshape(SUBLANES, n_slabs, SLAB_STATES).transpose(1, 0, 2)
    per_group_b = lambda b: b.reshape(S5_GROUP, groups, states).transpose(1, 0, 2)
    wb = jnp.concatenate([_slab_block_diag(per_group_b(bb_re)),
                          _slab_block_diag(per_group_b(bb_im))], axis=-1).astype(BF16)
    per_group_c = lambda c: c.astype(F32).transpose(0, 2, 1)
    wc_re = _slab_block_diag(per_group_c(c_re)).astype(BF16)
    wc_im = _slab_block_diag(per_group_c(c_im)).astype(BF16)
    y = _s5_scan(u, wb, wc_re, wc_im, to_slabs(pow_re), to_slabs(pow_im))
    return _s5_out(h, y, u, d_skip, w_glu, w_out)


def _rope(x, cos, sin_lo, sin_hi):
    return (x * cos + pltpu.roll(x, HEAD_DIM - ROT_HALF, 1) * sin_lo
            + pltpu.roll(x, ROT_HALF, 1) * sin_hi)


def _qkv_body(x_ref, g_ref, w_ref, qg_ref, kg_ref, invf_ref, q_ref, k_ref, v_ref):
    rows = x_ref.shape[0]
    width = q_ref.shape[1]
    xn = _rms_normed(x_ref[...], g_ref[...]).astype(BF16)

    pos = (pl.program_id(0) * rows
           + lax.broadcasted_iota(jnp.int32, (rows, HEAD_DIM), 0)).astype(F32)
    ang = pos * invf_ref[...]
    lane = lax.broadcasted_iota(jnp.int32, (rows, HEAD_DIM), 1)
    cos = jnp.cos(ang)
    sin = jnp.sin(ang)
    sin_lo = jnp.where(lane < ROT_HALF, -sin, 0.0)
    sin_hi = jnp.where((lane >= ROT_HALF) & (lane < ROT_DIM), sin, 0.0)

    def heads(part, gain, scale, out_ref):
        y = jnp.dot(xn, w_ref[:, part * width:(part + 1) * width], preferred_element_type=F32)
        for hd in range(width // HEAD_DIM):
            cols = slice(hd * HEAD_DIM, (hd + 1) * HEAD_DIM)
            yh = y[:, cols]
            if gain is not None:
                yh = _rope(_rms_normed(yh, gain), cos, sin_lo, sin_hi)
                if scale is not None:
                    yh = yh * scale
            out_ref[:, cols] = yh.astype(out_ref.dtype)

    heads(0, qg_ref[...], HEAD_DIM ** -0.5 * LOG2_E, q_ref)
    heads(1, kg_ref[...], None, k_ref)
    heads(2, None, None, v_ref)


def _qkv(h, gain, w_qkv, q_gain, k_gain):
    seq, d = h.shape
    width = w_qkv.shape[1] // 3
    inv_freq = ROPE_THETA ** (-jnp.arange(0, ROT_DIM, 2, dtype=F32) / ROT_DIM)
    invf = jnp.concatenate([inv_freq, inv_freq,
                            jnp.zeros((HEAD_DIM - ROT_DIM,), F32)]).reshape(1, HEAD_DIM)
    out = jax.ShapeDtypeStruct((seq, width), BF16)
    row_spec = lambda n: pl.BlockSpec((PROJ_ROWS, n), lambda i: (i, 0))
    const = lambda shape: pl.BlockSpec(shape, lambda i: (0, 0))
    return pl.pallas_call(
        _qkv_body,
        out_shape=(out, out, out),
        grid=(seq // PROJ_ROWS,),
        in_specs=[row_spec(d), const((1, d)), const((d, 3 * width)),
                  const((1, HEAD_DIM)), const((1, HEAD_DIM)), const((1, HEAD_DIM))],
        out_specs=(row_spec(width), row_spec(width), row_spec(width)),
        compiler_params=_params("arbitrary"),
        name="moba_qkv",
    )(h, gain.reshape(1, d), w_qkv.astype(BF16),
      q_gain.astype(F32).reshape(1, HEAD_DIM), k_gain.astype(F32).reshape(1, HEAD_DIM), invf)


def _attn_body(q_ref, k_ref, vt_ref, o_ref, kmean_ref, bias_ref, s_ref, mx_ref, p_ref):
    own = pl.program_id(1)
    n_blocks = k_ref.shape[0] // MOBA_BLOCK
    neg_inf = -jnp.inf

    @pl.when(own == 0)
    def _():
        for b in range(n_blocks):
            kb = k_ref[b * MOBA_BLOCK:(b + 1) * MOBA_BLOCK, :].astype(F32)
            kmean_ref[b:b + 1, :] = jnp.mean(kb, axis=0, keepdims=True)

    q = q_ref[...]
    nt = (((1,), (1,)), ((), ()))

    gate = lax.dot_general(kmean_ref[...].astype(BF16), q, nt, preferred_element_type=F32)
    blk = lax.broadcasted_iota(jnp.int32, gate.shape, 0).astype(F32)
    gate = jnp.where(blk < own.astype(F32), gate, neg_inf)
    selected = jnp.zeros(gate.shape, F32)
    for j in range(min(MOBA_TOPK, n_blocks)):
        top = jnp.max(gate, axis=0, keepdims=True)
        first = jnp.min(jnp.where(gate == top, blk, float(n_blocks)), axis=0, keepdims=True)
        hit = blk == first
        selected = jnp.maximum(selected, jnp.where(hit, jnp.where(own > j, 1.0, 0.0), 0.0))
        gate = jnp.where(hit, neg_inf, gate)
    bias_ref[...] = jnp.where(selected > 0.0, 0.0, neg_inf)

    def scores(b):
        kb0 = pl.multiple_of(b * MOBA_BLOCK, MOBA_BLOCK)
        return lax.dot_general(k_ref[pl.ds(kb0, MOBA_BLOCK), :], q, nt,
                               preferred_element_type=F32)

    s = scores(own)
    k_pos = lax.broadcasted_iota(jnp.int32, s.shape, 0)
    q_pos = lax.broadcasted_iota(jnp.int32, s.shape, 1)
    s = jnp.where(k_pos <= q_pos, s, neg_inf)
    m = jnp.max(s, axis=0, keepdims=True)
    p = jnp.exp2(s - m)
    l = jnp.sum(p, axis=0, keepdims=True)
    acc = jnp.dot(vt_ref[own], p.astype(BF16), preferred_element_type=F32)

    s_ref[...] = jnp.full(s_ref.shape, neg_inf, F32)
    mx_ref[...] = jnp.full(mx_ref.shape, neg_inf, F32)
    p_ref[...] = jnp.zeros(p_ref.shape, BF16)

    def block_of(group, g):
        return jnp.clip(group * ATTN_GROUP + g, 0, n_blocks - 1)

    def trip(i, carry):
        m, l, acc, alpha_prev = carry
        pv = jnp.dot(vt_ref[block_of(i - 2, 0)], p_ref[0], preferred_element_type=F32)
        for g in range(1, ATTN_GROUP):
            pv = pv + jnp.dot(vt_ref[block_of(i - 2, g)], p_ref[g], preferred_element_type=F32)
        acc = alpha_prev * acc + pv

        m_new = m
        for g in range(ATTN_GROUP):
            m_new = jnp.maximum(m_new, mx_ref[g:g + 1, :])
        alpha = jnp.exp2(m - m_new)
        l = alpha * l
        for g in range(ATTN_GROUP):
            p = jnp.exp2(s_ref[g] - m_new)
            l = l + jnp.sum(p, axis=0, keepdims=True)
            p_ref[g] = p.astype(BF16)

        for g in range(ATTN_GROUP):
            b = block_of(i, g)
            s = scores(b) + bias_ref[pl.ds(b, 1), :]
            s_ref[g] = s
            mx_ref[g:g + 1, :] = jnp.max(s, axis=0, keepdims=True)
        return m_new, l, acc, alpha

    n_groups = (own + ATTN_GROUP - 1) // ATTN_GROUP
    m, l, acc, alpha = lax.fori_loop(0, n_groups + 2, trip, (m, l, acc, jnp.ones_like(m)))
    o_ref[...] = (acc / l).T.astype(o_ref.dtype)


def _attention(q, k, vt):
    seq, width = q.shape
    n_heads = width // HEAD_DIM
    n_blocks = seq // MOBA_BLOCK
    assert n_blocks % ATTN_GROUP == 0 or n_blocks < ATTN_GROUP
    tile_spec = pl.BlockSpec((MOBA_BLOCK, HEAD_DIM), lambda hd, t: (t, hd))
    return pl.pallas_call(
        _attn_body,
        out_shape=jax.ShapeDtypeStruct((seq, width), BF16),
        grid=(n_heads, n_blocks),
        in_specs=[
            tile_spec,
            pl.BlockSpec((seq, HEAD_DIM), lambda hd, t: (0, hd)),
            pl.BlockSpec((None, n_blocks, HEAD_DIM, MOBA_BLOCK), lambda hd, t: (hd, 0, 0, 0)),
        ],
        out_specs=tile_spec,
        scratch_shapes=[pltpu.VMEM((n_blocks, HEAD_DIM), F32),
                        pltpu.VMEM((n_blocks, MOBA_BLOCK), F32),
                        pltpu.VMEM((ATTN_GROUP, MOBA_BLOCK, MOBA_BLOCK), F32),
                        pltpu.VMEM((ATTN_GROUP, MOBA_BLOCK), F32),
                        pltpu.VMEM((ATTN_GROUP, MOBA_BLOCK, MOBA_BLOCK), BF16)],
        compiler_params=_params("arbitrary", "arbitrary"),
        name="moba_attn",
    )(q, k, vt)


def _moba_mixer(h, gain, w_qkv, q_gain, k_gain, w_out):
    seq = h.shape[0]
    q, k, v = _qkv(h, gain, w_qkv, q_gain, k_gain)
    n_heads = v.shape[1] // HEAD_DIM
    vt = v.reshape(seq // MOBA_BLOCK, MOBA_BLOCK, n_heads, HEAD_DIM).transpose(2, 0, 3, 1)
    o = _attention(q, k, vt)
    return _proj_res(h, o, w_out)


def kernel(x, ffn_norm, ffn_w_gate, ffn_w_up, ffn_w_down, mix_norm, s5_w_in, s5_a_re, s5_a_im, s5_log_dt, s5_b_re, s5_b_im, s5_c_re, s5_c_im, s5_d, s5_w_glu, s5_w_out, moba_w_qkv, moba_q_norm, moba_k_norm, moba_w_out):
    bsz, seq, d = x.shape
    depth = ffn_norm.shape[0]
    outs = []
    for b in range(bsz):
        h = x[b].astype(F32)
        for layer in range(depth):
            h = _ffn(h, ffn_norm[layer, 0], ffn_w_gate[layer, 0], ffn_w_up[layer, 0],
                     ffn_w_down[layer, 0])
            i = layer // 2
            if layer % 2 == 0:
                h = _s5_mixer(h, mix_norm[layer], s5_w_in[i], s5_a_re[i], s5_a_im[i],
                              s5_log_dt[i], s5_b_re[i], s5_b_im[i], s5_c_re[i], s5_c_im[i],
                              s5_d[i], s5_w_glu[i], s5_w_out[i])
            else:
                h = _moba_mixer(h, mix_norm[layer], moba_w_qkv[i], moba_q_norm[i],
                                moba_k_norm[i], moba_w_out[i])
            h = _ffn(h, ffn_norm[layer, 1], ffn_w_gate[layer, 1], ffn_w_up[layer, 1],
                     ffn_w_down[layer, 1])
        outs.append(h)
    return jnp.stack(outs).astype(x.dtype)
```

```python
import functools

import jax
import jax.numpy as jnp
from jax import lax
from jax.experimental import pallas as pl
from jax.experimental.pallas import tpu as pltpu

F32 = jnp.float32
BF16 = jnp.bfloat16

LANES = 128
SUBLANES = 8
VMEM_LIMIT_BYTES = 56 * 1024 * 1024

EPS = 1e-6
MACARON_WEIGHT = 0.5
S5_GROUP = 16
S5_STATE = 64
GROUPS_PER_SLAB = LANES // S5_GROUP
SLAB_STATES = GROUPS_PER_SLAB * S5_STATE
N_HEADS = 8
HEAD_DIM = 128
MOBA_BLOCK = 256
MOBA_TOPK = 3
ROPE_THETA = 500000.0
ROT_DIM = HEAD_DIM // 4
ROT_HALF = ROT_DIM // 2

FFN_ROWS = 512
FFN_CHUNK = 256
PROJ_ROWS = 512
SCAN_ROWS = 256
ATTN_TILE_BLOCKS = 2
ATTN_GROUP = 8
LOG2_E = 1.4426950408889634


def _params(*semantics):
    return pltpu.CompilerParams(dimension_semantics=semantics,
                                vmem_limit_bytes=VMEM_LIMIT_BYTES)


def _rms_normed(x, gain):
    ms = jnp.mean(x * x, axis=-1, keepdims=True)
    return x * lax.rsqrt(ms + EPS) * gain


def _ffn_body(x_ref, g_ref, wg_ref, wu_ref, wd_ref, o_ref, act_ref):
    x = x_ref[...]
    xn = _rms_normed(x, g_ref[...]).astype(BF16)
    d_ff = wg_ref.shape[1]
    for j in range(d_ff // FFN_CHUNK):
        cols = slice(j * FFN_CHUNK, (j + 1) * FFN_CHUNK)
        gate = jnp.dot(xn, wg_ref[:, cols], preferred_element_type=F32)
        up = jnp.dot(xn, wu_ref[:, cols], preferred_element_type=F32)
        act_ref[:, cols] = (gate * jax.nn.sigmoid(gate) * up).astype(BF16)
    y = jnp.dot(act_ref[...], wd_ref[...], preferred_element_type=F32)
    o_ref[...] = x + MACARON_WEIGHT * y


def _ffn(h, gain, w_gate, w_up, w_down):
    seq, d = h.shape
    d_ff = w_gate.shape[1]
    return pl.pallas_call(
        _ffn_body,
        out_shape=jax.ShapeDtypeStruct((seq, d), F32),
        grid=(seq // FFN_ROWS,),
        in_specs=[
            pl.BlockSpec((FFN_ROWS, d), lambda i: (i, 0)),
            pl.BlockSpec((1, d), lambda i: (0, 0)),
            pl.BlockSpec((d, d_ff), lambda i: (0, 0)),
            pl.BlockSpec((d, d_ff), lambda i: (0, 0)),
            pl.BlockSpec((d_ff, d), lambda i: (0, 0)),
        ],
        out_specs=pl.BlockSpec((FFN_ROWS, d), lambda i: (i, 0)),
        scratch_shapes=[pltpu.VMEM((FFN_ROWS, d_ff), BF16)],
        compiler_params=_params("arbitrary"),
        name="ffn",
    )(h, gain.reshape(1, d), w_gate.astype(BF16), w_up.astype(BF16), w_down.astype(BF16))


def _norm_proj_body(x_ref, g_ref, w_ref, o_ref):
    xn = _rms_normed(x_ref[...], g_ref[...]).astype(BF16)
    o_ref[...] = jnp.dot(xn, w_ref[...], preferred_element_type=F32)


def _norm_proj(h, gain, w):
    seq, d = h.shape
    n = w.shape[1]
    return pl.pallas_call(
        _norm_proj_body,
        out_shape=jax.ShapeDtypeStruct((seq, n), F32),
        grid=(seq // PROJ_ROWS,),
        in_specs=[
            pl.BlockSpec((PROJ_ROWS, d), lambda i: (i, 0)),
            pl.BlockSpec((1, d), lambda i: (0, 0)),
            pl.BlockSpec((d, n), lambda i: (0, 0)),
        ],
        out_specs=pl.BlockSpec((PROJ_ROWS, n), lambda i: (i, 0)),
        compiler_params=_params("arbitrary"),
        name="norm_proj",
    )(h, gain.reshape(1, d), w.astype(BF16))


def _proj_res_body(h_ref, a_ref, w_ref, o_ref):
    o_ref[...] = h_ref[...] + jnp.dot(a_ref[...], w_ref[...], preferred_element_type=F32)


def _proj_res(h, a, w):
    seq, d = h.shape
    k = a.shape[1]
    return pl.pallas_call(
        _proj_res_body,
        out_shape=jax.ShapeDtypeStruct((seq, d), F32),
        grid=(seq // PROJ_ROWS,),
        in_specs=[
            pl.BlockSpec((PROJ_ROWS, d), lambda i: (i, 0)),
            pl.BlockSpec((PROJ_ROWS, k), lambda i: (i, 0)),
            pl.BlockSpec((k, d), lambda i: (0, 0)),
        ],
        out_specs=pl.BlockSpec((PROJ_ROWS, d), lambda i: (i, 0)),
        compiler_params=_params("arbitrary"),
        name="proj_res",
    )(h, a, w.astype(BF16))


def _s5_prep_body(are_ref, aim_ref, ldt_ref, bre_ref, bim_ref,
                  pre_ref, pim_ref, bbre_ref, bbim_ref):
    a_re = are_ref[...]
    a_im = aim_ref[...]
    dt = jnp.exp(ldt_ref[...])
    mag = jnp.exp(a_re * dt)
    lam_re = mag * jnp.cos(a_im * dt)
    lam_im = mag * jnp.sin(a_im * dt)
    den = a_re * a_re + a_im * a_im
    num_re = lam_re - 1.0
    coef_re = (num_re * a_re + lam_im * a_im) / den
    coef_im = (lam_im * a_re - num_re * a_im) / den
    b_re = bre_ref[...]
    b_im = bim_ref[...]
    bbre_ref[...] = coef_re * b_re - coef_im * b_im
    bbim_ref[...] = coef_re * b_im + coef_im * b_re
    p_re, p_im = lam_re, lam_im
    for r in range(SUBLANES):
        pre_ref[r:r + 1, :] = p_re
        pim_ref[r:r + 1, :] = p_im
        p_re, p_im = p_re * lam_re - p_im * lam_im, p_re * lam_im + p_im * lam_re


def _s5_prep(a_re, a_im, log_dt, b_re, b_im):
    groups, states = a_re.shape
    n_flat = groups * states
    row = lambda a: a.astype(F32).reshape(1, n_flat)
    ldt = jnp.broadcast_to(log_dt.astype(F32)[:, None], (groups, states)).reshape(1, n_flat)
    chan_rows = lambda b: b.astype(F32).transpose(2, 0, 1).reshape(S5_GROUP, n_flat)
    return pl.pallas_call(
        _s5_prep_body,
        out_shape=(
            jax.ShapeDtypeStruct((SUBLANES, n_flat), F32),
            jax.ShapeDtypeStruct((SUBLANES, n_flat), F32),
            jax.ShapeDtypeStruct((S5_GROUP, n_flat), F32),
            jax.ShapeDtypeStruct((S5_GROUP, n_flat), F32),
        ),
        name="s5_prep",
    )(row(a_re), row(a_im), ldt, chan_rows(b_re), chan_rows(b_im))


def _s5_scan_body(u_ref, wb_ref, wcre_ref, wcim_ref, pre_ref, pim_ref, y_ref,
                  bu_ref, st_ref, carry_ref):
    t_idx = pl.program_id(1)
    ns = SLAB_STATES

    @pl.when(t_idx == 0)
    def _():
        carry_ref[...] = jnp.zeros_like(carry_ref)

    bu_ref[...] = jnp.dot(u_ref[...].astype(BF16), wb_ref[0], preferred_element_type=F32)

    p_re = pre_ref[0]
    p_im = pim_ref[0]
    row = lax.broadcasted_iota(jnp.int32, (SUBLANES, ns), 0)
    steps = []
    for s in (1, 2, 4):
        keep = row >= s
        steps.append((s,
                      jnp.where(keep, p_re[s - 1:s, :], 0.0),
                      jnp.where(keep, p_im[s - 1:s, :], 0.0)))

    def tile_step(i, carry):
        c_re, c_im = carry
        r0 = pl.multiple_of(i * SUBLANES, SUBLANES)
        x_re = bu_ref[pl.ds(r0, SUBLANES), 0:ns]
        x_im = bu_ref[pl.ds(r0, SUBLANES), ns:2 * ns]
        for s, a_re, a_im in steps:
            s_re = pltpu.roll(x_re, s, 0)
            s_im = pltpu.roll(x_im, s, 0)
            x_re, x_im = (x_re + (a_re * s_re - a_im * s_im),
                          x_im + (a_re * s_im + a_im * s_re))
        x_re, x_im = (x_re + (p_re * c_re - p_im * c_im),
                      x_im + (p_re * c_im + p_im * c_re))
        st_ref[pl.ds(r0, SUBLANES), 0:ns] = x_re
        st_ref[pl.ds(r0, SUBLANES), ns:2 * ns] = x_im
        last = SUBLANES - 1
        return (jnp.broadcast_to(x_re[last:last + 1, :], (SUBLANES, ns)),
                jnp.broadcast_to(x_im[last:last + 1, :], (SUBLANES, ns)))

    n_tiles = u_ref.shape[0] // SUBLANES
    c_re, c_im = lax.fori_loop(0, n_tiles, tile_step,
                               (carry_ref[:, 0:ns], carry_ref[:, ns:2 * ns]), unroll=2)
    carry_ref[:, 0:ns] = c_re
    carry_ref[:, ns:2 * ns] = c_im

    y_ref[...] = (
        jnp.dot(st_ref[:, 0:ns].astype(BF16), wcre_ref[0], preferred_element_type=F32)
        - jnp.dot(st_ref[:, ns:2 * ns].astype(BF16), wcim_ref[0], preferred_element_type=F32))


def _s5_scan(u, wb, wc_re, wc_im, pow_re, pow_im):
    seq, width = u.shape
    n_slabs = width // LANES
    ns = SLAB_STATES
    return pl.pallas_call(
        _s5_scan_body,
        out_shape=jax.ShapeDtypeStruct((seq, width), F32),
        grid=(n_slabs, seq // SCAN_ROWS),
        in_specs=[
            pl.BlockSpec((SCAN_ROWS, LANES), lambda b, t: (t, b)),
            pl.BlockSpec((1, LANES, 2 * ns), lambda b, t: (b, 0, 0)),
            pl.BlockSpec((1, ns, LANES), lambda b, t: (b, 0, 0)),
            pl.BlockSpec((1, ns, LANES), lambda b, t: (b, 0, 0)),
            pl.BlockSpec((1, SUBLANES, ns), lambda b, t: (b, 0, 0)),
            pl.BlockSpec((1, SUBLANES, ns), lambda b, t: (b, 0, 0)),
        ],
        out_specs=pl.BlockSpec((SCAN_ROWS, LANES), lambda b, t: (t, b)),
        scratch_shapes=[
            pltpu.VMEM((SCAN_ROWS, 2 * ns), F32),
            pltpu.VMEM((SCAN_ROWS, 2 * ns), F32),
            pltpu.VMEM((SUBLANES, 2 * ns), F32),
        ],
        compiler_params=_params("arbitrary", "arbitrary"),
        name="s5_scan",
    )(u, wb, wc_re, wc_im, pow_re, pow_im)


def _slab_block_diag(w):
    groups, a, b = w.shape
    n_slabs = groups // GROUPS_PER_SLAB
    eye = jnp.eye(GROUPS_PER_SLAB, dtype=w.dtype)
    w4 = w.reshape(n_slabs, GROUPS_PER_SLAB, a, b)
    out = jnp.einsum('jgab,gh->jgahb', w4, eye)
    return out.reshape(n_slabs, GROUPS_PER_SLAB * a, GROUPS_PER_SLAB * b)


def _s5_out_body(h_ref, y_ref, u_ref, d_ref, wglu_ref, wout_ref, o_ref):
    g = jax.nn.gelu(y_ref[...] + d_ref[...] * u_ref[...])
    gate = jnp.dot(g.astype(BF16), wglu_ref[...], preferred_element_type=F32)
    g = g * jax.nn.sigmoid(gate)
    o_ref[...] = h_ref[...] + jnp.dot(g.astype(BF16), wout_ref[...], preferred_element_type=F32)


def _s5_out(h, y, u, d_skip, w_glu, w_out):
    seq, d = h.shape
    width = u.shape[1]
    row_spec = lambda n: pl.BlockSpec((PROJ_ROWS, n), lambda i: (i, 0))
    return pl.pallas_call(
        _s5_out_body,
        out_shape=jax.ShapeDtypeStruct((seq, d), F32),
        grid=(seq // PROJ_ROWS,),
        in_specs=[
            row_spec(d), row_spec(width), row_spec(width),
            pl.BlockSpec((1, width), lambda i: (0, 0)),
            pl.BlockSpec((width, width), lambda i: (0, 0)),
            pl.BlockSpec((width, d), lambda i: (0, 0)),
        ],
        out_specs=row_spec(d),
        compiler_params=_params("arbitrary"),
        name="s5_out",
    )(h, y, u, d_skip.astype(F32).reshape(1, width), w_glu.astype(BF16), w_out.astype(BF16))


def _s5_mixer(h, gain, w_in, a_re, a_im, log_dt, b_re, b_im, c_re, c_im, d_skip, w_glu, w_out):
    groups, states = a_re.shape
    u = _norm_proj(h, gain, w_in)
    pow_re, pow_im, bb_re, bb_im = _s5_prep(a_re, a_im, log_dt, b_re, b_im)
    n_slabs = groups // GROUPS_PER_SLAB
    to_slabs = lambda p: p.reshape(SUBLANES, n_slabs, SLAB_STATES).transpose(1, 0, 2)
    per_group_b = lambda b: b.reshape(S5_GROUP, groups, states).transpose(1, 0, 2)
    wb = jnp.concatenate([_slab_block_diag(per_group_b(bb_re)),
                          _slab_block_diag(per_group_b(bb_im))], axis=-1).astype(BF16)
    per_group_c = lambda c: c.astype(F32).transpose(0, 2, 1)
    wc_re = _slab_block_diag(per_group_c(c_re)).astype(BF16)
    wc_im = _slab_block_diag(per_group_c(c_im)).astype(BF16)
    y = _s5_scan(u, wb, wc_re, wc_im, to_slabs(pow_re), to_slabs(pow_im))
    return _s5_out(h, y, u, d_skip, w_glu, w_out)


def _rope(x, cos, sin_lo, sin_hi):
    return (x * cos + pltpu.roll(x, HEAD_DIM - ROT_HALF, 1) * sin_lo
            + pltpu.roll(x, ROT_HALF, 1) * sin_hi)


def _qkv_body(x_ref, g_ref, w_ref, wvt_ref, qg_ref, kg_ref, invf_ref, q_ref, k_ref, vt_ref,
              cos_rel_ref, sin_rel_ref):
    rows = x_ref.shape[0]
    width = q_ref.shape[1]
    xn = _rms_normed(x_ref[...], g_ref[...]).astype(BF16)

    vt = lax.dot_general(wvt_ref[...], xn, (((1,), (1,)), ((), ())), preferred_element_type=F32)
    vt_ref[...] = vt.astype(vt_ref.dtype)

    @pl.when(pl.program_id(0) == 0)
    def _():
        rel = lax.broadcasted_iota(jnp.int32, (rows, HEAD_DIM), 0).astype(F32) * invf_ref[...]
        cos_rel_ref[...] = jnp.cos(rel)
        sin_rel_ref[...] = jnp.sin(rel)

    base = (pl.program_id(0) * rows).astype(F32) * invf_ref[...]
    cos0, sin0 = jnp.cos(base), jnp.sin(base)
    cos = cos0 * cos_rel_ref[...] - sin0 * sin_rel_ref[...]
    sin = sin0 * cos_rel_ref[...] + cos0 * sin_rel_ref[...]
    lane = lax.broadcasted_iota(jnp.int32, (rows, HEAD_DIM), 1)
    sin_lo = jnp.where(lane < ROT_HALF, -sin, 0.0)
    sin_hi = jnp.where((lane >= ROT_HALF) & (lane < ROT_DIM), sin, 0.0)

    def heads(part, gain, scale, out_ref):
        y = jnp.dot(xn, w_ref[:, part * width:(part + 1) * width], preferred_element_type=F32)
        for hd in range(width // HEAD_DIM):
            cols = slice(hd * HEAD_DIM, (hd + 1) * HEAD_DIM)
            yh = _rope(_rms_normed(y[:, cols], gain), cos, sin_lo, sin_hi)
            if scale is not None:
                yh = yh * scale
            out_ref[:, cols] = yh.astype(out_ref.dtype)

    heads(0, qg_ref[...], HEAD_DIM ** -0.5 * LOG2_E, q_ref)
    heads(1, kg_ref[...], None, k_ref)


def _qkv(h, gain, w_qkv, q_gain, k_gain):
    seq, d = h.shape
    width = w_qkv.shape[1] // 3
    inv_freq = ROPE_THETA ** (-jnp.arange(0, ROT_DIM, 2, dtype=F32) / ROT_DIM)
    invf = jnp.concatenate([inv_freq, inv_freq,
                            jnp.zeros((HEAD_DIM - ROT_DIM,), F32)]).reshape(1, HEAD_DIM)
    out = jax.ShapeDtypeStruct((seq, width), BF16)
    row_spec = lambda n: pl.BlockSpec((PROJ_ROWS, n), lambda i: (i, 0))
    const = lambda shape: pl.BlockSpec(shape, lambda i: (0, 0))
    w_qk = w_qkv[:, :2 * width].astype(BF16)
    w_vt = w_qkv[:, 2 * width:].T.astype(BF16)
    return pl.pallas_call(
        _qkv_body,
        out_shape=(out, out, jax.ShapeDtypeStruct((width, seq), BF16)),
        grid=(seq // PROJ_ROWS,),
        in_specs=[row_spec(d), const((1, d)), const((d, 2 * width)), const((width, d)),
                  const((1, HEAD_DIM)), const((1, HEAD_DIM)), const((1, HEAD_DIM))],
        out_specs=(row_spec(width), row_spec(width),
                   pl.BlockSpec((width, PROJ_ROWS), lambda i: (0, i))),
        scratch_shapes=[pltpu.VMEM((PROJ_ROWS, HEAD_DIM), F32),
                        pltpu.VMEM((PROJ_ROWS, HEAD_DIM), F32)],
        compiler_params=_params("arbitrary"),
        name="moba_qkv",
    )(h, gain.reshape(1, d), w_qk, w_vt,
      q_gain.astype(F32).reshape(1, HEAD_DIM), k_gain.astype(F32).reshape(1, HEAD_DIM), invf)


def _attn_body(q_ref, k_ref, vt_ref, o_ref, kmean_ref, bias_ref, s_ref, mx_ref, p_ref):
    tile = pl.program_id(1)
    n_blocks = k_ref.shape[0] // MOBA_BLOCK
    n_q = q_ref.shape[0]
    top_block = tile * ATTN_TILE_BLOCKS + (ATTN_TILE_BLOCKS - 1)
    neg_inf = -jnp.inf

    @pl.when(tile == 0)
    def _():
        for b in range(n_blocks):
            kb = k_ref[b * MOBA_BLOCK:(b + 1) * MOBA_BLOCK, :].astype(F32)
            kmean_ref[b:b + 1, :] = jnp.mean(kb, axis=0, keepdims=True)

    q = q_ref[...]
    nt = (((1,), (1,)), ((), ()))

    def block_of(group, g):
        return top_block - (group * ATTN_GROUP + g)

    def scores(block):
        kb0 = pl.multiple_of(jnp.maximum(block, 0) * MOBA_BLOCK, MOBA_BLOCK)
        return lax.dot_general(k_ref[pl.ds(kb0, MOBA_BLOCK), :], q, nt,
                               preferred_element_type=F32)

    def stage_scores(group, causal):
        for g in range(ATTN_GROUP):
            s = scores(block_of(group, g))
            if causal and g < ATTN_TILE_BLOCKS:
                k_pos = lax.broadcasted_iota(jnp.int32, s.shape, 0)
                q_pos = lax.broadcasted_iota(jnp.int32, s.shape, 1)
                q_rel = q_pos - (ATTN_TILE_BLOCKS - 1 - g) * MOBA_BLOCK
                s = jnp.where((q_rel >= MOBA_BLOCK) | (k_pos <= q_rel), s, neg_inf)
            s_ref[g] = s
            mx_ref[g:g + 1, :] = jnp.max(s, axis=0, keepdims=True)

    def stage_values(group, acc, alpha):
        pv = None
        for g in range(ATTN_GROUP):
            b = jnp.clip(block_of(group, g), 0, n_blocks - 1)
            kb0 = pl.multiple_of(b * MOBA_BLOCK, MOBA_BLOCK)
            d = jnp.dot(vt_ref[:, pl.ds(kb0, MOBA_BLOCK)], p_ref[g], preferred_element_type=F32)
            pv = d if pv is None else pv + d
        return alpha * acc + pv

    p_ref[...] = jnp.zeros(p_ref.shape, BF16)
    stage_scores(0, causal=True)

    gate = lax.dot_general(kmean_ref[...].astype(BF16), q, nt, preferred_element_type=F32)
    blk = lax.broadcasted_iota(jnp.int32, gate.shape, 0)
    q_lane = lax.broadcasted_iota(jnp.int32, gate.shape, 1)
    own = tile * ATTN_TILE_BLOCKS + sum(
        (q_lane >= j * MOBA_BLOCK).astype(jnp.int32) for j in range(1, ATTN_TILE_BLOCKS))
    blk_f = blk.astype(F32)
    gate = jnp.where(blk < own, gate, neg_inf)
    selected = blk == own
    for j in range(min(MOBA_TOPK, n_blocks)):
        top = jnp.max(gate, axis=0, keepdims=True)
        first = jnp.min(jnp.where(gate == top, blk_f, float(n_blocks)), axis=0, keepdims=True)
        hit = blk_f == first
        selected = selected | (hit & (own > j))
        gate = jnp.where(hit, neg_inf, gate)
    bias_ref[0:n_blocks, :] = jnp.where(selected, 0.0, neg_inf)
    bias_ref[n_blocks:, :] = jnp.full((bias_ref.shape[0] - n_blocks, n_q), neg_inf, F32)

    def bias_row(group, g):
        b = block_of(group, g)
        return bias_ref[pl.ds(jnp.where(b < 0, n_blocks, b), 1), :]

    def trip(i, carry):
        m, l, acc, alpha_prev = carry
        acc = stage_values(i - 1, acc, alpha_prev)

        rows = [bias_row(i, g) for g in range(ATTN_GROUP)]
        m_new = m
        for g in range(ATTN_GROUP):
            m_new = jnp.maximum(m_new, mx_ref[g:g + 1, :] + rows[g])
        alpha = jnp.exp2(m - m_new)
        l = alpha * l
        for g in range(ATTN_GROUP):
            p = jnp.exp2(s_ref[g] - (m_new - rows[g]))
            l = l + jnp.sum(p, axis=0, keepdims=True)
            p_ref[g] = p.astype(BF16)

        stage_scores(i + 1, causal=False)
        return m_new, l, acc, alpha

    n_groups = (top_block + ATTN_GROUP) // ATTN_GROUP
    init = (jnp.full((1, n_q), neg_inf, F32), jnp.zeros((1, n_q), F32),
            jnp.zeros((HEAD_DIM, n_q), F32), jnp.ones((1, n_q), F32))
    m, l, acc, alpha = lax.fori_loop(0, n_groups, trip, init)
    acc = stage_values(n_groups - 1, acc, alpha)
    o_ref[...] = (acc / l).T.astype(o_ref.dtype)


def _attention(q, k, vt):
    seq, width = q.shape
    n_heads = width // HEAD_DIM
    n_blocks = seq // MOBA_BLOCK
    n_q = ATTN_TILE_BLOCKS * MOBA_BLOCK
    bias_rows = n_blocks + SUBLANES
    tile_spec = pl.BlockSpec((n_q, HEAD_DIM), lambda hd, t: (t, hd))
    return pl.pallas_call(
        _attn_body,
        out_shape=jax.ShapeDtypeStruct((seq, width), BF16),
        grid=(n_heads, seq // n_q),
        in_specs=[
            tile_spec,
            pl.BlockSpec((seq, HEAD_DIM), lambda hd, t: (0, hd)),
            pl.BlockSpec((HEAD_DIM, seq), lambda hd, t: (hd, 0)),
        ],
        out_specs=tile_spec,
        scratch_shapes=[pltpu.VMEM((n_blocks, HEAD_DIM), F32),
                        pltpu.VMEM((bias_rows, n_q), F32),
                        pltpu.VMEM((ATTN_GROUP, MOBA_BLOCK, n_q), F32),
                        pltpu.VMEM((ATTN_GROUP, n_q), F32),
                        pltpu.VMEM((ATTN_GROUP, MOBA_BLOCK, n_q), BF16)],
        compiler_params=_params("arbitrary", "arbitrary"),
        name="moba_attn",
    )(q, k, vt)


def _moba_mixer(h, gain, w_qkv, q_gain, k_gain, w_out):
    q, k, vt = _qkv(h, gain, w_qkv, q_gain, k_gain)
    o = _attention(q, k, vt)
    return _proj_res(h, o, w_out)


def kernel(x, ffn_norm, ffn_w_gate, ffn_w_up, ffn_w_down, mix_norm, s5_w_in, s5_a_re, s5_a_im, s5_log_dt, s5_b_re, s5_b_im, s5_c_re, s5_c_im, s5_d, s5_w_glu, s5_w_out, moba_w_qkv, moba_q_norm, moba_k_norm, moba_w_out):
    bsz, seq, d = x.shape
    depth = ffn_norm.shape[0]
    outs = []
    for b in range(bsz):
        h = x[b].astype(F32)
        for layer in range(depth):
            h = _ffn(h, ffn_norm[layer, 0], ffn_w_gate[layer, 0], ffn_w_up[layer, 0],
                     ffn_w_down[layer, 0])
            i = layer // 2
            if layer % 2 == 0:
                h = _s5_mixer(h, mix_norm[layer], s5_w_in[i], s5_a_re[i], s5_a_im[i],
                              s5_log_dt[i], s5_b_re[i], s5_b_im[i], s5_c_re[i], s5_c_im[i],
                              s5_d[i], s5_w_glu[i], s5_w_out[i])
            else:
                h = _moba_mixer(h, mix_norm[layer], moba_w_qkv[i], moba_q_norm[i],
                                moba_k_norm[i], moba_w_out[i])
            h = _ffn(h, ffn_norm[layer, 1], ffn_w_gate[layer, 1], ffn_w_up[layer, 1],
                     ffn_w_down[layer, 1])
        outs.append(h)
    return jnp.stack(outs).astype(x.dtype)
```

```python
import functools

import jax
import jax.numpy as jnp
from jax import lax
from jax.experimental import pallas as pl
from jax.experimental.pallas import tpu as pltpu

F32 = jnp.float32
BF16 = jnp.bfloat16

LANES = 128
SUBLANES = 8
VMEM_LIMIT_BYTES = 56 * 1024 * 1024

EPS = 1e-6
MACARON_WEIGHT = 0.5
S5_GROUP = 16
S5_STATE = 64
GROUPS_PER_SLAB = LANES // S5_GROUP
SLAB_STATES = GROUPS_PER_SLAB * S5_STATE
N_HEADS = 8
HEAD_DIM = 128
MOBA_BLOCK = 256
MOBA_TOPK = 3
ROPE_THETA = 500000.0
ROT_DIM = HEAD_DIM // 4
ROT_HALF = ROT_DIM // 2

FFN_ROWS = 512
FFN_CHUNK = 256
PROJ_ROWS = 512
SSM_ROWS = 2048
ATTN_TILE_BLOCKS = 2
ATTN_GROUP = 8
LOG2_E = 1.4426950408889634


def _params(*semantics):
    return pltpu.CompilerParams(dimension_semantics=semantics,
                                vmem_limit_bytes=VMEM_LIMIT_BYTES)


def _rms_normed(x, gain):
    ms = jnp.mean(x * x, axis=-1, keepdims=True)
    return x * lax.rsqrt(ms + EPS) * gain


def _ffn_body(x_ref, g_ref, wg_ref, wu_ref, wd_ref, o_ref, act_ref):
    x = x_ref[...]
    xn = _rms_normed(x, g_ref[...]).astype(BF16)
    d_ff = wg_ref.shape[1]
    for j in range(d_ff // FFN_CHUNK):
        cols = slice(j * FFN_CHUNK, (j + 1) * FFN_CHUNK)
        gate = jnp.dot(xn, wg_ref[:, cols], preferred_element_type=F32)
        up = jnp.dot(xn, wu_ref[:, cols], preferred_element_type=F32)
        act_ref[:, cols] = (gate * jax.nn.sigmoid(gate) * up).astype(BF16)
    y = jnp.dot(act_ref[...], wd_ref[...], preferred_element_type=F32)
    o_ref[...] = x + MACARON_WEIGHT * y


def _ffn(h, gain, w_gate, w_up, w_down, idx):
    seq, d = h.shape
    d_ff = w_gate.shape[2]
    return pl.pallas_call(
        _ffn_body,
        out_shape=jax.ShapeDtypeStruct((seq, d), F32),
        grid=(seq // FFN_ROWS,),
        in_specs=[
            pl.BlockSpec((FFN_ROWS, d), lambda i: (i, 0)),
            pl.BlockSpec((1, d), lambda i: (0, 0)),
            pl.BlockSpec((None, d, d_ff), lambda i: (idx, 0, 0)),
            pl.BlockSpec((None, d, d_ff), lambda i: (idx, 0, 0)),
            pl.BlockSpec((None, d_ff, d), lambda i: (idx, 0, 0)),
        ],
        out_specs=pl.BlockSpec((FFN_ROWS, d), lambda i: (i, 0)),
        scratch_shapes=[pltpu.VMEM((FFN_ROWS, d_ff), BF16)],
        compiler_params=_params("arbitrary"),
        name="ffn",
    )(h, gain.reshape(1, d), w_gate, w_up, w_down)


def _norm_proj_body(x_ref, g_ref, w_ref, o_ref):
    xn = _rms_normed(x_ref[...], g_ref[...]).astype(BF16)
    o_ref[...] = jnp.dot(xn, w_ref[...], preferred_element_type=F32)


def _norm_proj(h, gain, w):
    seq, d = h.shape
    n = w.shape[1]
    return pl.pallas_call(
        _norm_proj_body,
        out_shape=jax.ShapeDtypeStruct((seq, n), F32),
        grid=(seq // PROJ_ROWS,),
        in_specs=[
            pl.BlockSpec((PROJ_ROWS, d), lambda i: (i, 0)),
            pl.BlockSpec((1, d), lambda i: (0, 0)),
            pl.BlockSpec((d, n), lambda i: (0, 0)),
        ],
        out_specs=pl.BlockSpec((PROJ_ROWS, n), lambda i: (i, 0)),
        compiler_params=_params("arbitrary"),
        name="norm_proj",
    )(h, gain.reshape(1, d), w.astype(BF16))


def _proj_res_body(h_ref, a_ref, w_ref, o_ref):
    o_ref[...] = h_ref[...] + jnp.dot(a_ref[...], w_ref[...], preferred_element_type=F32)


def _proj_res(h, a, w):
    seq, d = h.shape
    k = a.shape[1]
    return pl.pallas_call(
        _proj_res_body,
        out_shape=jax.ShapeDtypeStruct((seq, d), F32),
        grid=(seq // PROJ_ROWS,),
        in_specs=[
            pl.BlockSpec((PROJ_ROWS, d), lambda i: (i, 0)),
            pl.BlockSpec((PROJ_ROWS, k), lambda i: (i, 0)),
            pl.BlockSpec((k, d), lambda i: (0, 0)),
        ],
        out_specs=pl.BlockSpec((PROJ_ROWS, d), lambda i: (i, 0)),
        compiler_params=_params("arbitrary"),
        name="proj_res",
    )(h, a, w.astype(BF16))


def _s5_prep_body(are_ref, aim_ref, ldt_ref, bre_ref, bim_ref,
                  pre_ref, pim_ref, bbre_ref, bbim_ref):
    a_re = are_ref[...]
    a_im = aim_ref[...]
    dt = jnp.exp(ldt_ref[...])
    mag = jnp.exp(a_re * dt)
    lam_re = mag * jnp.cos(a_im * dt)
    lam_im = mag * jnp.sin(a_im * dt)
    den = a_re * a_re + a_im * a_im
    num_re = lam_re - 1.0
    coef_re = (num_re * a_re + lam_im * a_im) / den
    coef_im = (lam_im * a_re - num_re * a_im) / den
    b_re = bre_ref[...]
    b_im = bim_ref[...]
    bbre_ref[...] = coef_re * b_re - coef_im * b_im
    bbim_ref[...] = coef_re * b_im + coef_im * b_re
    p_re, p_im = lam_re, lam_im
    for r in range(SUBLANES):
        pre_ref[r:r + 1, :] = p_re
        pim_ref[r:r + 1, :] = p_im
        p_re, p_im = p_re * lam_re - p_im * lam_im, p_re * lam_im + p_im * lam_re


def _s5_prep(a_re, a_im, log_dt, b_re, b_im):
    groups, states = a_re.shape
    n_flat = groups * states
    row = lambda a: a.astype(F32).reshape(1, n_flat)
    ldt = jnp.broadcast_to(log_dt.astype(F32)[:, None], (groups, states)).reshape(1, n_flat)
    chan_rows = lambda b: b.astype(F32).transpose(2, 0, 1).reshape(S5_GROUP, n_flat)
    return pl.pallas_call(
        _s5_prep_body,
        out_shape=(
            jax.ShapeDtypeStruct((SUBLANES, n_flat), F32),
            jax.ShapeDtypeStruct((SUBLANES, n_flat), F32),
            jax.ShapeDtypeStruct((S5_GROUP, n_flat), F32),
            jax.ShapeDtypeStruct((S5_GROUP, n_flat), F32),
        ),
        name="s5_prep",
    )(row(a_re), row(a_im), ldt, chan_rows(b_re), chan_rows(b_im))


def _s5_ops_body(wb_ref, wc_ref, wct_ref, pre_ref, pim_ref,
                 kcat_ref, wg_ref, wet_ref, p8re_ref, p8im_ref):
    ns = SLAB_STATES
    b_re, b_im = wb_ref[0, :, 0:ns], wb_ref[0, :, ns:2 * ns]
    c_re, c_im = wc_ref[0, 0:ns, :], wc_ref[0, ns:2 * ns, :]
    ct_re, ct_im = wct_ref[0, :, 0:ns], wct_ref[0, :, ns:2 * ns]
    powers = [(jnp.ones((1, ns), F32), jnp.zeros((1, ns), F32))]
    powers += [(pre_ref[0, r:r + 1, :], pim_ref[0, r:r + 1, :]) for r in range(SUBLANES)]
    hi = lax.Precision.HIGHEST
    for t in range(SUBLANES):
        l_re, l_im = powers[t]
        lb_re = b_re * l_re - b_im * l_im
        lb_im = b_re * l_im + b_im * l_re
        k_t = (jnp.dot(lb_re, c_re, precision=hi, preferred_element_type=F32)
               - jnp.dot(lb_im, c_im, precision=hi, preferred_element_type=F32))
        kcat_ref[0, t * LANES:(t + 1) * LANES, :] = k_t.astype(kcat_ref.dtype)
        k = SUBLANES - 1 - t
        wg_ref[0, k * LANES:(k + 1) * LANES, 0:ns] = lb_re.astype(wg_ref.dtype)
        wg_ref[0, k * LANES:(k + 1) * LANES, ns:2 * ns] = lb_im.astype(wg_ref.dtype)
    for k in range(SUBLANES):
        l_re, l_im = powers[k + 1]
        wet_ref[0, k * LANES:(k + 1) * LANES, 0:ns] = (ct_re * l_re - ct_im * l_im).astype(wet_ref.dtype)
        wet_ref[0, k * LANES:(k + 1) * LANES, ns:2 * ns] = (-(ct_re * l_im + ct_im * l_re)).astype(wet_ref.dtype)
    m_re, m_im = powers[SUBLANES]
    q_re, q_im = m_re, m_im
    for r in range(SUBLANES):
        p8re_ref[0, r:r + 1, :] = q_re
        p8im_ref[0, r:r + 1, :] = q_im
        q_re, q_im = q_re * m_re - q_im * m_im, q_re * m_im + q_im * m_re


def _s5_ops(wb, wc, wct, pow_re, pow_im):
    n_slabs = wb.shape[0]
    ns = SLAB_STATES
    wide = SUBLANES * LANES
    slab = lambda *shape: pl.BlockSpec((1,) + shape, lambda j: (j, 0, 0))
    return pl.pallas_call(
        _s5_ops_body,
        out_shape=(
            jax.ShapeDtypeStruct((n_slabs, wide, LANES), BF16),
            jax.ShapeDtypeStruct((n_slabs, wide, 2 * ns), BF16),
            jax.ShapeDtypeStruct((n_slabs, wide, 2 * ns), BF16),
            jax.ShapeDtypeStruct((n_slabs, SUBLANES, ns), F32),
            jax.ShapeDtypeStruct((n_slabs, SUBLANES, ns), F32),
        ),
        grid=(n_slabs,),
        in_specs=[slab(LANES, 2 * ns), slab(2 * ns, LANES), slab(LANES, 2 * ns),
                  slab(SUBLANES, ns), slab(SUBLANES, ns)],
        out_specs=(slab(wide, LANES), slab(wide, 2 * ns), slab(wide, 2 * ns),
                   slab(SUBLANES, ns), slab(SUBLANES, ns)),
        compiler_params=_params("arbitrary"),
        name="s5_ops",
    )(wb, wc, wct, pow_re, pow_im)


def _s5_ssm_body(u_ref, kcat_ref, wg_ref, we_ref, pre_ref, pim_ref, y_ref,
                 ucat_ref, g_ref, h_ref, carry_ref):
    t_idx = pl.program_id(1)
    ns = SLAB_STATES
    rows = u_ref.shape[0]
    n_chunks = rows // SUBLANES

    @pl.when(t_idx == 0)
    def _():
        carry_ref[...] = jnp.zeros_like(carry_ref)

    u3 = u_ref[...].reshape(n_chunks, SUBLANES, LANES)
    row = lax.broadcasted_iota(jnp.int32, u3.shape, 1)
    for tau in range(SUBLANES):
        shifted = u3 if tau == 0 else jnp.where(row >= tau, pltpu.roll(u3, tau, 1), 0.0)
        ucat_ref[:, tau * LANES:(tau + 1) * LANES] = shifted.reshape(rows, LANES).astype(BF16)
    y_ref[...] = jnp.dot(ucat_ref[...], kcat_ref[0], preferred_element_type=F32)

    per_step = [u_ref[pl.ds(k, n_chunks, stride=SUBLANES), :].astype(BF16)
                for k in range(SUBLANES)]
    g_ref[...] = jnp.dot(jnp.concatenate(per_step, axis=1), wg_ref[0],
                         preferred_element_type=F32)

    p_re = pre_ref[0]
    p_im = pim_ref[0]
    srow = lax.broadcasted_iota(jnp.int32, (SUBLANES, ns), 0)
    steps = []
    for s in (1, 2, 4):
        keep = srow >= s
        steps.append((s,
                      jnp.where(keep, p_re[s - 1:s, :], 0.0),
                      jnp.where(keep, p_im[s - 1:s, :], 0.0)))

    h_ref[0:SUBLANES, :] = carry_ref[...]

    def tile_step(i, carry):
        c_re, c_im = carry
        r0 = pl.multiple_of(i * SUBLANES, SUBLANES)
        x_re = g_ref[pl.ds(r0, SUBLANES), 0:ns]
        x_im = g_ref[pl.ds(r0, SUBLANES), ns:2 * ns]
        for s, a_re, a_im in steps:
            s_re = pltpu.roll(x_re, s, 0)
            s_im = pltpu.roll(x_im, s, 0)
            x_re, x_im = (x_re + (a_re * s_re - a_im * s_im),
                          x_im + (a_re * s_im + a_im * s_re))
        x_re, x_im = (x_re + (p_re * c_re - p_im * c_im),
                      x_im + (p_re * c_im + p_im * c_re))
        h_ref[pl.ds(r0 + SUBLANES, SUBLANES), 0:ns] = x_re
        h_ref[pl.ds(r0 + SUBLANES, SUBLANES), ns:2 * ns] = x_im
        last = SUBLANES - 1
        return (jnp.broadcast_to(x_re[last:last + 1, :], (SUBLANES, ns)),
                jnp.broadcast_to(x_im[last:last + 1, :], (SUBLANES, ns)))

    c_re, c_im = lax.fori_loop(0, n_chunks // SUBLANES, tile_step,
                               (carry_ref[:, 0:ns], carry_ref[:, ns:2 * ns]), unroll=2)
    carry_ref[:, 0:ns] = c_re
    carry_ref[:, ns:2 * ns] = c_im

    h_in = h_ref[pl.ds(SUBLANES - 1, n_chunks), :].astype(BF16)
    y2 = jnp.dot(h_in, we_ref[0], preferred_element_type=F32)
    for k in range(SUBLANES):
        y_ref[pl.ds(k, n_chunks, stride=SUBLANES), :] += y2[:, k * LANES:(k + 1) * LANES]


def _s5_ssm(u, kcat, wg, we, pow8_re, pow8_im):
    seq, width = u.shape
    n_slabs = width // LANES
    ns = SLAB_STATES
    wide = SUBLANES * LANES
    rows = min(SSM_ROWS, seq)
    n_chunks = rows // SUBLANES
    slab = lambda *shape: pl.BlockSpec((1,) + shape, lambda b, t: (b, 0, 0))
    return pl.pallas_call(
        _s5_ssm_body,
        out_shape=jax.ShapeDtypeStruct((seq, width), F32),
        grid=(n_slabs, seq // rows),
        in_specs=[
            pl.BlockSpec((rows, LANES), lambda b, t: (t, b)),
            slab(wide, LANES), slab(wide, 2 * ns), slab(2 * ns, wide),
            slab(SUBLANES, ns), slab(SUBLANES, ns),
        ],
        out_specs=pl.BlockSpec((rows, LANES), lambda b, t: (t, b)),
        scratch_shapes=[
            pltpu.VMEM((rows, wide), BF16),
            pltpu.VMEM((n_chunks, 2 * ns), F32),
            pltpu.VMEM((n_chunks + SUBLANES, 2 * ns), F32),
            pltpu.VMEM((SUBLANES, 2 * ns), F32),
        ],
        compiler_params=_params("arbitrary", "arbitrary"),
        name="s5_ssm",
    )(u, kcat, wg, we, pow8_re, pow8_im)


def _slab_block_diag(w):
    groups, a, b = w.shape
    n_slabs = groups // GROUPS_PER_SLAB
    eye = jnp.eye(GROUPS_PER_SLAB, dtype=w.dtype)
    w4 = w.reshape(n_slabs, GROUPS_PER_SLAB, a, b)
    out = jnp.einsum('jgab,gh->jgahb', w4, eye)
    return out.reshape(n_slabs, GROUPS_PER_SLAB * a, GROUPS_PER_SLAB * b)


def _s5_out_body(h_ref, y_ref, u_ref, d_ref, wglu_ref, wout_ref, o_ref):
    g = jax.nn.gelu(y_ref[...] + d_ref[...] * u_ref[...])
    gate = jnp.dot(g.astype(BF16), wglu_ref[...], preferred_element_type=F32)
    g = g * jax.nn.sigmoid(gate)
    o_ref[...] = h_ref[...] + jnp.dot(g.astype(BF16), wout_ref[...], preferred_element_type=F32)


def _s5_out(h, y, u, d_skip, w_glu, w_out):
    seq, d = h.shape
    width = u.shape[1]
    row_spec = lambda n: pl.BlockSpec((PROJ_ROWS, n), lambda i: (i, 0))
    return pl.pallas_call(
        _s5_out_body,
        out_shape=jax.ShapeDtypeStruct((seq, d), F32),
        grid=(seq // PROJ_ROWS,),
        in_specs=[
            row_spec(d), row_spec(width), row_spec(width),
            pl.BlockSpec((1, width), lambda i: (0, 0)),
            pl.BlockSpec((width, width), lambda i: (0, 0)),
            pl.BlockSpec((width, d), lambda i: (0, 0)),
        ],
        out_specs=row_spec(d),
        compiler_params=_params("arbitrary"),
        name="s5_out",
    )(h, y, u, d_skip.astype(F32).reshape(1, width), w_glu.astype(BF16), w_out.astype(BF16))


def _s5_mixer(h, gain, w_in, a_re, a_im, log_dt, b_re, b_im, c_re, c_im, d_skip, w_glu, w_out):
    groups, states = a_re.shape
    u = _norm_proj(h, gain, w_in)
    pow_re, pow_im, bb_re, bb_im = _s5_prep(a_re, a_im, log_dt, b_re, b_im)
    n_slabs = groups // GROUPS_PER_SLAB
    to_slabs = lambda p: p.reshape(SUBLANES, n_slabs, SLAB_STATES).transpose(1, 0, 2)
    per_group_b = lambda b: b.reshape(S5_GROUP, groups, states).transpose(1, 0, 2)
    wb = jnp.concatenate([_slab_block_diag(per_group_b(bb_re)),
                          _slab_block_diag(per_group_b(bb_im))], axis=-1)
    wct = jnp.concatenate([_slab_block_diag(c_re.astype(F32)),
                           _slab_block_diag(c_im.astype(F32))], axis=-1)
    wc = wct.transpose(0, 2, 1)
    kcat, wg, wet, pow8_re, pow8_im = _s5_ops(wb, wc, wct, to_slabs(pow_re), to_slabs(pow_im))
    y = _s5_ssm(u, kcat, wg, wet.transpose(0, 2, 1), pow8_re, pow8_im)
    return _s5_out(h, y, u, d_skip, w_glu, w_out)


def _rope(x, cos, sin_lo, sin_hi):
    return (x * cos + pltpu.roll(x, HEAD_DIM - ROT_HALF, 1) * sin_lo
            + pltpu.roll(x, ROT_HALF, 1) * sin_hi)


def _qkv_body(x_ref, g_ref, w_ref, wvt_ref, qg_ref, kg_ref, invf_ref, q_ref, k_ref, vt_ref,
              cos_rel_ref, sin_rel_ref):
    rows = x_ref.shape[0]
    width = q_ref.shape[1]
    xn = _rms_normed(x_ref[...], g_ref[...]).astype(BF16)

    vt = lax.dot_general(wvt_ref[...], xn, (((1,), (1,)), ((), ())), preferred_element_type=F32)
    vt_ref[...] = vt.astype(vt_ref.dtype)

    @pl.when(pl.program_id(0) == 0)
    def _():
        rel = lax.broadcasted_iota(jnp.int32, (rows, HEAD_DIM), 0).astype(F32) * invf_ref[...]
        cos_rel_ref[...] = jnp.cos(rel)
        sin_rel_ref[...] = jnp.sin(rel)

    base = (pl.program_id(0) * rows).astype(F32) * invf_ref[...]
    cos0, sin0 = jnp.cos(base), jnp.sin(base)
    cos = cos0 * cos_rel_ref[...] - sin0 * sin_rel_ref[...]
    sin = sin0 * cos_rel_ref[...] + cos0 * sin_rel_ref[...]
    lane = lax.broadcasted_iota(jnp.int32, (rows, HEAD_DIM), 1)
    sin_lo = jnp.where(lane < ROT_HALF, -sin, 0.0)
    sin_hi = jnp.where((lane >= ROT_HALF) & (lane < ROT_DIM), sin, 0.0)

    def heads(part, gain, scale, out_ref):
        y = jnp.dot(xn, w_ref[:, part * width:(part + 1) * width], preferred_element_type=F32)
        for hd in range(width // HEAD_DIM):
            cols = slice(hd * HEAD_DIM, (hd + 1) * HEAD_DIM)
            yh = _rope(_rms_normed(y[:, cols], gain), cos, sin_lo, sin_hi)
            if scale is not None:
                yh = yh * scale
            out_ref[:, cols] = yh.astype(out_ref.dtype)

    heads(0, qg_ref[...], HEAD_DIM ** -0.5 * LOG2_E, q_ref)
    heads(1, kg_ref[...], None, k_ref)


def _qkv(h, gain, w_qkv, q_gain, k_gain):
    seq, d = h.shape
    width = w_qkv.shape[1] // 3
    inv_freq = ROPE_THETA ** (-jnp.arange(0, ROT_DIM, 2, dtype=F32) / ROT_DIM)
    invf = jnp.concatenate([inv_freq, inv_freq,
                            jnp.zeros((HEAD_DIM - ROT_DIM,), F32)]).reshape(1, HEAD_DIM)
    out = jax.ShapeDtypeStruct((seq, width), BF16)
    row_spec = lambda n: pl.BlockSpec((PROJ_ROWS, n), lambda i: (i, 0))
    const = lambda shape: pl.BlockSpec(shape, lambda i: (0, 0))
    w_qk = w_qkv[:, :2 * width].astype(BF16)
    w_vt = w_qkv[:, 2 * width:].T.astype(BF16)
    return pl.pallas_call(
        _qkv_body,
        out_shape=(out, out, jax.ShapeDtypeStruct((width, seq), BF16)),
        grid=(seq // PROJ_ROWS,),
        in_specs=[row_spec(d), const((1, d)), const((d, 2 * width)), const((width, d)),
                  const((1, HEAD_DIM)), const((1, HEAD_DIM)), const((1, HEAD_DIM))],
        out_specs=(row_spec(width), row_spec(width),
                   pl.BlockSpec((width, PROJ_ROWS), lambda i: (0, i))),
        scratch_shapes=[pltpu.VMEM((PROJ_ROWS, HEAD_DIM), F32),
                        pltpu.VMEM((PROJ_ROWS, HEAD_DIM), F32)],
        compiler_params=_params("arbitrary"),
        name="moba_qkv",
    )(h, gain.reshape(1, d), w_qk, w_vt,
      q_gain.astype(F32).reshape(1, HEAD_DIM), k_gain.astype(F32).reshape(1, HEAD_DIM), invf)


def _attn_body(q_ref, k_ref, vt_ref, o_ref, kmean_ref, bias_ref, s_ref, mx_ref, p_ref):
    tile = pl.program_id(1)
    n_blocks = k_ref.shape[0] // MOBA_BLOCK
    n_q = q_ref.shape[0]
    top_block = tile * ATTN_TILE_BLOCKS + (ATTN_TILE_BLOCKS - 1)
    neg_inf = -jnp.inf

    @pl.when(tile == 0)
    def _():
        for b in range(n_blocks):
            kb = k_ref[b * MOBA_BLOCK:(b + 1) * MOBA_BLOCK, :].astype(F32)
            kmean_ref[b:b + 1, :] = jnp.mean(kb, axis=0, keepdims=True)

    q = q_ref[...]
    nt = (((1,), (1,)), ((), ()))

    def block_of(group, g):
        return top_block - (group * ATTN_GROUP + g)

    def scores(block):
        kb0 = pl.multiple_of(jnp.maximum(block, 0) * MOBA_BLOCK, MOBA_BLOCK)
        return lax.dot_general(k_ref[pl.ds(kb0, MOBA_BLOCK), :], q, nt,
                               preferred_element_type=F32)

    def stage_scores(group, causal):
        for g in range(ATTN_GROUP):
            s = scores(block_of(group, g))
            if causal and g < ATTN_TILE_BLOCKS:
                k_pos = lax.broadcasted_iota(jnp.int32, s.shape, 0)
                q_pos = lax.broadcasted_iota(jnp.int32, s.shape, 1)
                q_rel = q_pos - (ATTN_TILE_BLOCKS - 1 - g) * MOBA_BLOCK
                s = jnp.where((q_rel >= MOBA_BLOCK) | (k_pos <= q_rel), s, neg_inf)
            s_ref[g] = s
            mx_ref[g:g + 1, :] = jnp.max(s, axis=0, keepdims=True)

    def stage_values(group, acc, alpha):
        pv = None
        for g in range(ATTN_GROUP):
            b = jnp.clip(block_of(group, g), 0, n_blocks - 1)
            kb0 = pl.multiple_of(b * MOBA_BLOCK, MOBA_BLOCK)
            d = jnp.dot(vt_ref[:, pl.ds(kb0, MOBA_BLOCK)], p_ref[g], preferred_element_type=F32)
            pv = d if pv is None else pv + d
        return alpha * acc + pv

    p_ref[...] = jnp.zeros(p_ref.shape, BF16)
    stage_scores(0, causal=True)

    gate = lax.dot_general(kmean_ref[...].astype(BF16), q, nt, preferred_element_type=F32)
    blk = lax.broadcasted_iota(jnp.int32, gate.shape, 0)
    q_lane = lax.broadcasted_iota(jnp.int32, gate.shape, 1)
    own = tile * ATTN_TILE_BLOCKS + sum(
        (q_lane >= j * MOBA_BLOCK).astype(jnp.int32) for j in range(1, ATTN_TILE_BLOCKS))
    blk_f = blk.astype(F32)
    gate = jnp.where(blk < own, gate, neg_inf)
    selected = blk == own
    for j in range(min(MOBA_TOPK, n_blocks)):
        top = jnp.max(gate, axis=0, keepdims=True)
        first = jnp.min(jnp.where(gate == top, blk_f, float(n_blocks)), axis=0, keepdims=True)
        hit = blk_f == first
        selected = selected | (hit & (own > j))
        gate = jnp.where(hit, neg_inf, gate)
    bias_ref[0:n_blocks, :] = jnp.where(selected, 0.0, neg_inf)
    bias_ref[n_blocks:, :] = jnp.full((bias_ref.shape[0] - n_blocks, n_q), neg_inf, F32)

    def bias_row(group, g):
        b = block_of(group, g)
        return bias_ref[pl.ds(jnp.where(b < 0, n_blocks, b), 1), :]

    def trip(i, carry):
        m, l, acc, alpha_prev = carry
        acc = stage_values(i - 1, acc, alpha_prev)

        rows = [bias_row(i, g) for g in range(ATTN_GROUP)]
        m_new = m
        for g in range(ATTN_GROUP):
            m_new = jnp.maximum(m_new, mx_ref[g:g + 1, :] + rows[g])
        alpha = jnp.exp2(m - m_new)
        l = alpha * l
        for g in range(ATTN_GROUP):
            p = jnp.exp2(s_ref[g] - (m_new - rows[g]))
            l = l + jnp.sum(p, axis=0, keepdims=True)
            p_ref[g] = p.astype(BF16)

        stage_scores(i + 1, causal=False)
        return m_new, l, acc, alpha

    n_groups = (top_block + ATTN_GROUP) // ATTN_GROUP
    init = (jnp.full((1, n_q), neg_inf, F32), jnp.zeros((1, n_q), F32),
            jnp.zeros((HEAD_DIM, n_q), F32), jnp.ones((1, n_q), F32))
    m, l, acc, alpha = lax.fori_loop(0, n_groups, trip, init)
    acc = stage_values(n_groups - 1, acc, alpha)
    o_ref[...] = (acc / l).T.astype(o_ref.dtype)


def _attention(q, k, vt):
    seq, width = q.shape
    n_heads = width // HEAD_DIM
    n_blocks = seq // MOBA_BLOCK
    n_q = ATTN_TILE_BLOCKS * MOBA_BLOCK
    bias_rows = n_blocks + SUBLANES
    tile_spec = pl.BlockSpec((n_q, HEAD_DIM), lambda hd, t: (t, hd))
    return pl.pallas_call(
        _attn_body,
        out_shape=jax.ShapeDtypeStruct((seq, width), BF16),
        grid=(n_heads, seq // n_q),
        in_specs=[
            tile_spec,
            pl.BlockSpec((seq, HEAD_DIM), lambda hd, t: (0, hd)),
            pl.BlockSpec((HEAD_DIM, seq), lambda hd, t: (hd, 0)),
        ],
        out_specs=tile_spec,
        scratch_shapes=[pltpu.VMEM((n_blocks, HEAD_DIM), F32),
                        pltpu.VMEM((bias_rows, n_q), F32),
                        pltpu.VMEM((ATTN_GROUP, MOBA_BLOCK, n_q), F32),
                        pltpu.VMEM((ATTN_GROUP, n_q), F32),
                        pltpu.VMEM((ATTN_GROUP, MOBA_BLOCK, n_q), BF16)],
        compiler_params=_params("arbitrary", "arbitrary"),
        name="moba_attn",
    )(q, k, vt)


def _moba_mixer(h, gain, w_qkv, q_gain, k_gain, w_out):
    q, k, vt = _qkv(h, gain, w_qkv, q_gain, k_gain)
    o = _attention(q, k, vt)
    return _proj_res(h, o, w_out)


def kernel(x, ffn_norm, ffn_w_gate, ffn_w_up, ffn_w_down, mix_norm, s5_w_in, s5_a_re, s5_a_im, s5_log_dt, s5_b_re, s5_b_im, s5_c_re, s5_c_im, s5_d, s5_w_glu, s5_w_out, moba_w_qkv, moba_q_norm, moba_k_norm, moba_w_out):
    bsz, seq, d = x.shape
    depth = ffn_norm.shape[0]
    d_ff = ffn_w_gate.shape[-1]
    wg_all = ffn_w_gate.reshape(depth * 2, d, d_ff).astype(BF16)
    wu_all = ffn_w_up.reshape(depth * 2, d, d_ff).astype(BF16)
    wd_all = ffn_w_down.reshape(depth * 2, d_ff, d).astype(BF16)
    outs = []
    for b in range(bsz):
        h = (x.reshape(seq, d) if bsz == 1 else x[b]).astype(F32)
        for layer in range(depth):
            h = _ffn(h, ffn_norm[layer, 0], wg_all, wu_all, wd_all, 2 * layer)
            i = layer // 2
            if layer % 2 == 0:
                h = _s5_mixer(h, mix_norm[layer], s5_w_in[i], s5_a_re[i], s5_a_im[i],
                              s5_log_dt[i], s5_b_re[i], s5_b_im[i], s5_c_re[i], s5_c_im[i],
                              s5_d[i], s5_w_glu[i], s5_w_out[i])
            else:
                h = _moba_mixer(h, mix_norm[layer], moba_w_qkv[i], moba_q_norm[i],
                                moba_k_norm[i], moba_w_out[i])
            h = _ffn(h, ffn_norm[layer, 1], wg_all, wu_all, wd_all, 2 * layer + 1)
        outs.append(h)
    out = outs[0].reshape(1, seq, d) if bsz == 1 else jnp.stack(outs)
    return out.astype(x.dtype)
```

```python
import functools

import jax
import jax.numpy as jnp
from jax import lax
from jax.experimental import pallas as pl
from jax.experimental.pallas import tpu as pltpu

F32 = jnp.float32
BF16 = jnp.bfloat16

LANES = 128
SUBLANES = 8
VMEM_LIMIT_BYTES = 56 * 1024 * 1024

EPS = 1e-6
MACARON_WEIGHT = 0.5
S5_GROUP = 16
S5_STATE = 64
GROUPS_PER_SLAB = LANES // S5_GROUP
SLAB_STATES = GROUPS_PER_SLAB * S5_STATE
N_HEADS = 8
HEAD_DIM = 128
MOBA_BLOCK = 256
MOBA_TOPK = 3
ROPE_THETA = 500000.0
ROT_DIM = HEAD_DIM // 4
ROT_HALF = ROT_DIM // 2

FFN_ROWS = 512
FFN_CHUNK = 256
PROJ_ROWS = 512
SSM_ROWS = 2048
ATTN_TILE_BLOCKS = 2
ATTN_GROUP = 8
LOG2_E = 1.4426950408889634


def _params(*semantics):
    return pltpu.CompilerParams(dimension_semantics=semantics,
                                vmem_limit_bytes=VMEM_LIMIT_BYTES)


def _rms_normed(x, gain):
    ms = jnp.mean(x * x, axis=-1, keepdims=True)
    return x * lax.rsqrt(ms + EPS) * gain


def _ffn_body(x_ref, g_ref, wg_ref, wu_ref, wd_ref, o_ref, act_ref):
    x = x_ref[...]
    xn = _rms_normed(x, g_ref[...]).astype(BF16)
    d_ff = wg_ref.shape[1]
    for j in range(d_ff // FFN_CHUNK):
        cols = slice(j * FFN_CHUNK, (j + 1) * FFN_CHUNK)
        gate = jnp.dot(xn, wg_ref[:, cols], preferred_element_type=F32)
        up = jnp.dot(xn, wu_ref[:, cols], preferred_element_type=F32)
        act_ref[:, cols] = (gate * jax.nn.sigmoid(gate) * up).astype(BF16)
    y = jnp.dot(act_ref[...], wd_ref[...], preferred_element_type=F32)
    o_ref[...] = x + MACARON_WEIGHT * y


def _ffn(h, gain, w_gate, w_up, w_down, idx):
    seq, d = h.shape
    d_ff = w_gate.shape[2]
    return pl.pallas_call(
        _ffn_body,
        out_shape=jax.ShapeDtypeStruct((seq, d), F32),
        grid=(seq // FFN_ROWS,),
        in_specs=[
            pl.BlockSpec((FFN_ROWS, d), lambda i: (i, 0)),
            pl.BlockSpec((1, d), lambda i: (0, 0)),
            pl.BlockSpec((None, d, d_ff), lambda i: (idx, 0, 0)),
            pl.BlockSpec((None, d, d_ff), lambda i: (idx, 0, 0)),
            pl.BlockSpec((None, d_ff, d), lambda i: (idx, 0, 0)),
        ],
        out_specs=pl.BlockSpec((FFN_ROWS, d), lambda i: (i, 0)),
        scratch_shapes=[pltpu.VMEM((FFN_ROWS, d_ff), BF16)],
        compiler_params=_params("arbitrary"),
        name="ffn",
    )(h, gain.reshape(1, d), w_gate, w_up, w_down)


def _norm_proj_body(x_ref, g_ref, w_ref, o_ref):
    xn = _rms_normed(x_ref[...], g_ref[...]).astype(BF16)
    o_ref[...] = jnp.dot(xn, w_ref[...], preferred_element_type=F32)


def _norm_proj(h, gain, w):
    seq, d = h.shape
    n = w.shape[1]
    return pl.pallas_call(
        _norm_proj_body,
        out_shape=jax.ShapeDtypeStruct((seq, n), F32),
        grid=(seq // PROJ_ROWS,),
        in_specs=[
            pl.BlockSpec((PROJ_ROWS, d), lambda i: (i, 0)),
            pl.BlockSpec((1, d), lambda i: (0, 0)),
            pl.BlockSpec((d, n), lambda i: (0, 0)),
        ],
        out_specs=pl.BlockSpec((PROJ_ROWS, n), lambda i: (i, 0)),
        compiler_params=_params("arbitrary"),
        name="norm_proj",
    )(h, gain.reshape(1, d), w.astype(BF16))


def _proj_res_body(h_ref, a_ref, w_ref, o_ref):
    o_ref[...] = h_ref[...] + jnp.dot(a_ref[...], w_ref[...], preferred_element_type=F32)


def _proj_res(h, a, w):
    seq, d = h.shape
    k = a.shape[1]
    return pl.pallas_call(
        _proj_res_body,
        out_shape=jax.ShapeDtypeStruct((seq, d), F32),
        grid=(seq // PROJ_ROWS,),
        in_specs=[
            pl.BlockSpec((PROJ_ROWS, d), lambda i: (i, 0)),
            pl.BlockSpec((PROJ_ROWS, k), lambda i: (i, 0)),
            pl.BlockSpec((k, d), lambda i: (0, 0)),
        ],
        out_specs=pl.BlockSpec((PROJ_ROWS, d), lambda i: (i, 0)),
        compiler_params=_params("arbitrary"),
        name="proj_res",
    )(h, a, w.astype(BF16))


def _s5_prep_body(are_ref, aim_ref, ldt_ref, bre_ref, bim_ref,
                  pre_ref, pim_ref, bbre_ref, bbim_ref):
    a_re = are_ref[...]
    a_im = aim_ref[...]
    dt = jnp.exp(ldt_ref[...])
    mag = jnp.exp(a_re * dt)
    lam_re = mag * jnp.cos(a_im * dt)
    lam_im = mag * jnp.sin(a_im * dt)
    den = a_re * a_re + a_im * a_im
    num_re = lam_re - 1.0
    coef_re = (num_re * a_re + lam_im * a_im) / den
    coef_im = (lam_im * a_re - num_re * a_im) / den
    b_re = bre_ref[...]
    b_im = bim_ref[...]
    bbre_ref[...] = coef_re * b_re - coef_im * b_im
    bbim_ref[...] = coef_re * b_im + coef_im * b_re
    p_re, p_im = lam_re, lam_im
    for r in range(SUBLANES):
        pre_ref[r:r + 1, :] = p_re
        pim_ref[r:r + 1, :] = p_im
        p_re, p_im = p_re * lam_re - p_im * lam_im, p_re * lam_im + p_im * lam_re


def _s5_prep(a_re, a_im, log_dt, b_re, b_im):
    groups, states = a_re.shape
    n_flat = groups * states
    row = lambda a: a.astype(F32).reshape(1, n_flat)
    ldt = jnp.broadcast_to(log_dt.astype(F32)[:, None], (groups, states)).reshape(1, n_flat)
    chan_rows = lambda b: b.astype(F32).transpose(2, 0, 1).reshape(S5_GROUP, n_flat)
    return pl.pallas_call(
        _s5_prep_body,
        out_shape=(
            jax.ShapeDtypeStruct((SUBLANES, n_flat), F32),
            jax.ShapeDtypeStruct((SUBLANES, n_flat), F32),
            jax.ShapeDtypeStruct((S5_GROUP, n_flat), F32),
            jax.ShapeDtypeStruct((S5_GROUP, n_flat), F32),
        ),
        name="s5_prep",
    )(row(a_re), row(a_im), ldt, chan_rows(b_re), chan_rows(b_im))


def _s5_ops_body(wb_ref, wc_ref, wct_ref, pre_ref, pim_ref,
                 kcat_ref, wg_ref, wet_ref, p8re_ref, p8im_ref):
    ns = SLAB_STATES
    b_re, b_im = wb_ref[0, :, 0:ns], wb_ref[0, :, ns:2 * ns]
    c_re, c_im = wc_ref[0, 0:ns, :], wc_ref[0, ns:2 * ns, :]
    ct_re, ct_im = wct_ref[0, :, 0:ns], wct_ref[0, :, ns:2 * ns]
    powers = [(jnp.ones((1, ns), F32), jnp.zeros((1, ns), F32))]
    powers += [(pre_ref[0, r:r + 1, :], pim_ref[0, r:r + 1, :]) for r in range(SUBLANES)]
    hi = lax.Precision.HIGHEST
    for t in range(SUBLANES):
        l_re, l_im = powers[t]
        lb_re = b_re * l_re - b_im * l_im
        lb_im = b_re * l_im + b_im * l_re
        k_t = (jnp.dot(lb_re, c_re, precision=hi, preferred_element_type=F32)
               - jnp.dot(lb_im, c_im, precision=hi, preferred_element_type=F32))
        kcat_ref[0, t * LANES:(t + 1) * LANES, :] = k_t.astype(kcat_ref.dtype)
        k = SUBLANES - 1 - t
        wg_ref[0, k * LANES:(k + 1) * LANES, 0:ns] = lb_re.astype(wg_ref.dtype)
        wg_ref[0, k * LANES:(k + 1) * LANES, ns:2 * ns] = lb_im.astype(wg_ref.dtype)
    for k in range(SUBLANES):
        l_re, l_im = powers[k + 1]
        wet_ref[0, k * LANES:(k + 1) * LANES, 0:ns] = (ct_re * l_re - ct_im * l_im).astype(wet_ref.dtype)
        wet_ref[0, k * LANES:(k + 1) * LANES, ns:2 * ns] = (-(ct_re * l_im + ct_im * l_re)).astype(wet_ref.dtype)
    m_re, m_im = powers[SUBLANES]
    q_re, q_im = m_re, m_im
    for r in range(SUBLANES):
        p8re_ref[0, r:r + 1, :] = q_re
        p8im_ref[0, r:r + 1, :] = q_im
        q_re, q_im = q_re * m_re - q_im * m_im, q_re * m_im + q_im * m_re


def _s5_ops(wb, wc, wct, pow_re, pow_im):
    n_slabs = wb.shape[0]
    ns = SLAB_STATES
    wide = SUBLANES * LANES
    slab = lambda *shape: pl.BlockSpec((1,) + shape, lambda j: (j, 0, 0))
    return pl.pallas_call(
        _s5_ops_body,
        out_shape=(
            jax.ShapeDtypeStruct((n_slabs, wide, LANES), BF16),
            jax.ShapeDtypeStruct((n_slabs, wide, 2 * ns), BF16),
            jax.ShapeDtypeStruct((n_slabs, wide, 2 * ns), BF16),
            jax.ShapeDtypeStruct((n_slabs, SUBLANES, ns), F32),
            jax.ShapeDtypeStruct((n_slabs, SUBLANES, ns), F32),
        ),
        grid=(n_slabs,),
        in_specs=[slab(LANES, 2 * ns), slab(2 * ns, LANES), slab(LANES, 2 * ns),
                  slab(SUBLANES, ns), slab(SUBLANES, ns)],
        out_specs=(slab(wide, LANES), slab(wide, 2 * ns), slab(wide, 2 * ns),
                   slab(SUBLANES, ns), slab(SUBLANES, ns)),
        compiler_params=_params("arbitrary"),
        name="s5_ops",
    )(wb, wc, wct, pow_re, pow_im)


def _s5_ssm_body(u_ref, kcat_ref, wg_ref, we_ref, pre_ref, pim_ref, y_ref,
                 ucat_ref, g_ref, h_ref, carry_ref):
    t_idx = pl.program_id(1)
    ns = SLAB_STATES
    rows = u_ref.shape[0]
    n_chunks = rows // SUBLANES

    @pl.when(t_idx == 0)
    def _():
        carry_ref[...] = jnp.zeros_like(carry_ref)

    u3 = u_ref[...].reshape(n_chunks, SUBLANES, LANES)
    row = lax.broadcasted_iota(jnp.int32, u3.shape, 1)
    for tau in range(SUBLANES):
        shifted = u3 if tau == 0 else jnp.where(row >= tau, pltpu.roll(u3, tau, 1), 0.0)
        ucat_ref[:, tau * LANES:(tau + 1) * LANES] = shifted.reshape(rows, LANES).astype(BF16)
    y_ref[...] = jnp.dot(ucat_ref[...], kcat_ref[0], preferred_element_type=F32)

    per_step = [u_ref[pl.ds(k, n_chunks, stride=SUBLANES), :].astype(BF16)
                for k in range(SUBLANES)]
    g_ref[...] = jnp.dot(jnp.concatenate(per_step, axis=1), wg_ref[0],
                         preferred_element_type=F32)

    p_re = pre_ref[0]
    p_im = pim_ref[0]
    srow = lax.broadcasted_iota(jnp.int32, (SUBLANES, ns), 0)
    steps = []
    for s in (1, 2, 4):
        keep = srow >= s
        steps.append((s,
                      jnp.where(keep, p_re[s - 1:s, :], 0.0),
                      jnp.where(keep, p_im[s - 1:s, :], 0.0)))

    h_ref[0:SUBLANES, :] = carry_ref[...]

    def tile_step(i, carry):
        c_re, c_im = carry
        r0 = pl.multiple_of(i * SUBLANES, SUBLANES)
        x_re = g_ref[pl.ds(r0, SUBLANES), 0:ns]
        x_im = g_ref[pl.ds(r0, SUBLANES), ns:2 * ns]
        for s, a_re, a_im in steps:
            s_re = pltpu.roll(x_re, s, 0)
            s_im = pltpu.roll(x_im, s, 0)
            x_re, x_im = (x_re + (a_re * s_re - a_im * s_im),
                          x_im + (a_re * s_im + a_im * s_re))
        x_re, x_im = (x_re + (p_re * c_re - p_im * c_im),
                      x_im + (p_re * c_im + p_im * c_re))
        h_ref[pl.ds(r0 + SUBLANES, SUBLANES), 0:ns] = x_re
        h_ref[pl.ds(r0 + SUBLANES, SUBLANES), ns:2 * ns] = x_im
        last = SUBLANES - 1
        return (jnp.broadcast_to(x_re[last:last + 1, :], (SUBLANES, ns)),
                jnp.broadcast_to(x_im[last:last + 1, :], (SUBLANES, ns)))

    c_re, c_im = lax.fori_loop(0, n_chunks // SUBLANES, tile_step,
                               (carry_ref[:, 0:ns], carry_ref[:, ns:2 * ns]), unroll=2)
    carry_ref[:, 0:ns] = c_re
    carry_ref[:, ns:2 * ns] = c_im

    h_in = h_ref[pl.ds(SUBLANES - 1, n_chunks), :].astype(BF16)
    y2 = jnp.dot(h_in, we_ref[0], preferred_element_type=F32)
    for k in range(SUBLANES):
        y_ref[pl.ds(k, n_chunks, stride=SUBLANES), :] += y2[:, k * LANES:(k + 1) * LANES]


def _s5_ssm(u, kcat, wg, we, pow8_re, pow8_im):
    seq, width = u.shape
    n_slabs = width // LANES
    ns = SLAB_STATES
    wide = SUBLANES * LANES
    rows = min(SSM_ROWS, seq)
    n_chunks = rows // SUBLANES
    slab = lambda *shape: pl.BlockSpec((1,) + shape, lambda b, t: (b, 0, 0))
    return pl.pallas_call(
        _s5_ssm_body,
        out_shape=jax.ShapeDtypeStruct((seq, width), F32),
        grid=(n_slabs, seq // rows),
        in_specs=[
            pl.BlockSpec((rows, LANES), lambda b, t: (t, b)),
            slab(wide, LANES), slab(wide, 2 * ns), slab(2 * ns, wide),
            slab(SUBLANES, ns), slab(SUBLANES, ns),
        ],
        out_specs=pl.BlockSpec((rows, LANES), lambda b, t: (t, b)),
        scratch_shapes=[
            pltpu.VMEM((rows, wide), BF16),
            pltpu.VMEM((n_chunks, 2 * ns), F32),
            pltpu.VMEM((n_chunks + SUBLANES, 2 * ns), F32),
            pltpu.VMEM((SUBLANES, 2 * ns), F32),
        ],
        compiler_params=_params("arbitrary", "arbitrary"),
        name="s5_ssm",
    )(u, kcat, wg, we, pow8_re, pow8_im)


def _slab_block_diag(w):
    groups, a, b = w.shape
    n_slabs = groups // GROUPS_PER_SLAB
    eye = jnp.eye(GROUPS_PER_SLAB, dtype=w.dtype)
    w4 = w.reshape(n_slabs, GROUPS_PER_SLAB, a, b)
    out = jnp.einsum('jgab,gh->jgahb', w4, eye)
    return out.reshape(n_slabs, GROUPS_PER_SLAB * a, GROUPS_PER_SLAB * b)


def _s5_out_body(h_ref, y_ref, u_ref, d_ref, wglu_ref, wout_ref, o_ref):
    g = jax.nn.gelu(y_ref[...] + d_ref[...] * u_ref[...])
    gate = jnp.dot(g.astype(BF16), wglu_ref[...], preferred_element_type=F32)
    g = g * jax.nn.sigmoid(gate)
    o_ref[...] = h_ref[...] + jnp.dot(g.astype(BF16), wout_ref[...], preferred_element_type=F32)


def _s5_out(h, y, u, d_skip, w_glu, w_out):
    seq, d = h.shape
    width = u.shape[1]
    row_spec = lambda n: pl.BlockSpec((PROJ_ROWS, n), lambda i: (i, 0))
    return pl.pallas_call(
        _s5_out_body,
        out_shape=jax.ShapeDtypeStruct((seq, d), F32),
        grid=(seq // PROJ_ROWS,),
        in_specs=[
            row_spec(d), row_spec(width), row_spec(width),
            pl.BlockSpec((1, width), lambda i: (0, 0)),
            pl.BlockSpec((width, width), lambda i: (0, 0)),
            pl.BlockSpec((width, d), lambda i: (0, 0)),
        ],
        out_specs=row_spec(d),
        compiler_params=_params("arbitrary"),
        name="s5_out",
    )(h, y, u, d_skip.astype(F32).reshape(1, width), w_glu.astype(BF16), w_out.astype(BF16))


def _s5_mixer(h, gain, w_in, a_re, a_im, log_dt, b_re, b_im, c_re, c_im, d_skip, w_glu, w_out):
    groups, states = a_re.shape
    u = _norm_proj(h, gain, w_in)
    pow_re, pow_im, bb_re, bb_im = _s5_prep(a_re, a_im, log_dt, b_re, b_im)
    n_slabs = groups // GROUPS_PER_SLAB
    to_slabs = lambda p: p.reshape(SUBLANES, n_slabs, SLAB_STATES).transpose(1, 0, 2)
    per_group_b = lambda b: b.reshape(S5_GROUP, groups, states).transpose(1, 0, 2)
    wb = jnp.concatenate([_slab_block_diag(per_group_b(bb_re)),
                          _slab_block_diag(per_group_b(bb_im))], axis=-1)
    wct = jnp.concatenate([_slab_block_diag(c_re.astype(F32)),
                           _slab_block_diag(c_im.astype(F32))], axis=-1)
    wc = wct.transpose(0, 2, 1)
    kcat, wg, wet, pow8_re, pow8_im = _s5_ops(wb, wc, wct, to_slabs(pow_re), to_slabs(pow_im))
    y = _s5_ssm(u, kcat, wg, wet.transpose(0, 2, 1), pow8_re, pow8_im)
    return _s5_out(h, y, u, d_skip, w_glu, w_out)


def _head_dim_order():
    idx = list(range(ROT_HALF)) + list(range(ROT_DIM, HEAD_DIM // 2 + ROT_HALF))
    idx += list(range(ROT_HALF, ROT_DIM)) + list(range(HEAD_DIM // 2 + ROT_HALF, HEAD_DIM))
    return jnp.asarray(idx, jnp.int32)


def _rope(x, cos, sin_signed):
    return x * cos + pltpu.roll(x, HEAD_DIM // 2, 1) * sin_signed


def _qkv_body(x_ref, g_ref, w_ref, wvt_ref, qg_ref, kg_ref, invf_ref, q_ref, k_ref, vt_ref,
              cos_rel_ref, sin_rel_ref, y_ref):
    step = pl.program_id(0)
    rows = x_ref.shape[0]
    width = q_ref.shape[1]

    @pl.when(step == 0)
    def _():
        rel = lax.broadcasted_iota(jnp.int32, (rows, HEAD_DIM), 0).astype(F32) * invf_ref[...]
        cos_rel_ref[...] = jnp.cos(rel)
        sin_rel_ref[...] = jnp.sin(rel)
        y_ref[...] = jnp.zeros_like(y_ref)

    base = (jnp.maximum(step - 1, 0) * rows).astype(F32) * invf_ref[...]
    cos0, sin0 = jnp.cos(base), jnp.sin(base)
    cos = cos0 * cos_rel_ref[...] - sin0 * sin_rel_ref[...]
    sin = sin0 * cos_rel_ref[...] + cos0 * sin_rel_ref[...]
    lane = lax.broadcasted_iota(jnp.int32, (rows, HEAD_DIM), 1)
    half = HEAD_DIM // 2
    sin_signed = jnp.where(lane < ROT_HALF, -sin,
                           jnp.where((lane >= half) & (lane < half + ROT_HALF), sin, 0.0))

    def finish(part, gain, scale, out_ref):
        for hd in range(width // HEAD_DIM):
            cols = slice(hd * HEAD_DIM, (hd + 1) * HEAD_DIM)
            yh = y_ref[:, part * width + hd * HEAD_DIM:part * width + (hd + 1) * HEAD_DIM]
            yh = _rope(_rms_normed(yh, gain), cos, sin_signed)
            if scale is not None:
                yh = yh * scale
            out_ref[:, cols] = yh.astype(out_ref.dtype)

    finish(0, qg_ref[...], HEAD_DIM ** -0.5 * LOG2_E, q_ref)
    finish(1, kg_ref[...], None, k_ref)

    xn = _rms_normed(x_ref[...], g_ref[...]).astype(BF16)
    vt = lax.dot_general(wvt_ref[...], xn, (((1,), (1,)), ((), ())), preferred_element_type=F32)
    vt_ref[...] = vt.astype(vt_ref.dtype)
    y_ref[...] = jnp.dot(xn, w_ref[...], preferred_element_type=F32)


def _qkv(h, gain, w_qkv, q_gain, k_gain):
    seq, d = h.shape
    width = w_qkv.shape[1] // 3
    inv_freq = ROPE_THETA ** (-jnp.arange(0, ROT_DIM, 2, dtype=F32) / ROT_DIM)
    gap = jnp.zeros((HEAD_DIM // 2 - ROT_HALF,), F32)
    invf = jnp.concatenate([inv_freq, gap, inv_freq, gap]).reshape(1, HEAD_DIM)
    out = jax.ShapeDtypeStruct((seq, width), BF16)
    n_tiles = seq // PROJ_ROWS
    cur_rows = lambda n: pl.BlockSpec((PROJ_ROWS, n), lambda i: (jnp.minimum(i, n_tiles - 1), 0))
    prev_rows = lambda n: pl.BlockSpec((PROJ_ROWS, n), lambda i: (jnp.maximum(i - 1, 0), 0))
    const = lambda shape: pl.BlockSpec(shape, lambda i: (0, 0))
    order = _head_dim_order()
    cols = (jnp.arange(2 * width // HEAD_DIM, dtype=jnp.int32)[:, None] * HEAD_DIM
            + order[None, :]).reshape(-1)
    w_qk = w_qkv[:, :2 * width][:, cols].astype(BF16)
    w_vt = w_qkv[:, 2 * width:].T.astype(BF16)
    q_gain, k_gain = q_gain[order], k_gain[order]
    return pl.pallas_call(
        _qkv_body,
        out_shape=(out, out, jax.ShapeDtypeStruct((width, seq), BF16)),
        grid=(n_tiles + 1,),
        in_specs=[cur_rows(d), const((1, d)), const((d, 2 * width)), const((width, d)),
                  const((1, HEAD_DIM)), const((1, HEAD_DIM)), const((1, HEAD_DIM))],
        out_specs=(prev_rows(width), prev_rows(width),
                   pl.BlockSpec((width, PROJ_ROWS), lambda i: (0, jnp.minimum(i, n_tiles - 1)))),
        scratch_shapes=[pltpu.VMEM((PROJ_ROWS, HEAD_DIM), F32),
                        pltpu.VMEM((PROJ_ROWS, HEAD_DIM), F32),
                        pltpu.VMEM((PROJ_ROWS, 2 * width), F32)],
        compiler_params=_params("arbitrary"),
        name="moba_qkv",
    )(h, gain.reshape(1, d), w_qk, w_vt,
      q_gain.astype(F32).reshape(1, HEAD_DIM), k_gain.astype(F32).reshape(1, HEAD_DIM), invf)


def _attn_body(q_ref, k_ref, vt_ref, o_ref, kmean_ref, bias_ref, s_ref, mx_ref, p_ref,
               acc_ref, l_ref, alpha_ref):
    step = pl.program_id(1)
    n_tiles = pl.num_programs(1) - 1
    tile = jnp.minimum(step, n_tiles - 1)
    n_blocks = k_ref.shape[0] // MOBA_BLOCK
    n_q = q_ref.shape[0]
    top_block = tile * ATTN_TILE_BLOCKS + (ATTN_TILE_BLOCKS - 1)
    neg_inf = -jnp.inf

    @pl.when((pl.program_id(0) == 0) & (step == 0))
    def _():
        p_ref[...] = jnp.zeros(p_ref.shape, BF16)
        acc_ref[...] = jnp.zeros(acc_ref.shape, F32)
        l_ref[...] = jnp.ones(l_ref.shape, F32)
        alpha_ref[...] = jnp.ones(alpha_ref.shape, F32)

    @pl.when(step == 0)
    def _():
        for b in range(n_blocks):
            kb = k_ref[b * MOBA_BLOCK:(b + 1) * MOBA_BLOCK, :].astype(F32)
            kmean_ref[b:b + 1, :] = jnp.mean(kb, axis=0, keepdims=True)

    q = q_ref[...]
    nt = (((1,), (1,)), ((), ()))

    def groups_of(top):
        return (top + ATTN_GROUP) // ATTN_GROUP

    def block_of(group, g, top=top_block):
        return top - (group * ATTN_GROUP + g)

    def scores(block):
        kb0 = pl.multiple_of(jnp.maximum(block, 0) * MOBA_BLOCK, MOBA_BLOCK)
        return lax.dot_general(k_ref[pl.ds(kb0, MOBA_BLOCK), :], q, nt,
                               preferred_element_type=F32)

    def stage_scores(group, causal):
        for g in range(ATTN_GROUP):
            s = scores(block_of(group, g))
            if causal and g < ATTN_TILE_BLOCKS:
                k_pos = lax.broadcasted_iota(jnp.int32, s.shape, 0)
                q_pos = lax.broadcasted_iota(jnp.int32, s.shape, 1)
                q_rel = q_pos - (ATTN_TILE_BLOCKS - 1 - g) * MOBA_BLOCK
                s = jnp.where((q_rel >= MOBA_BLOCK) | (k_pos <= q_rel), s, neg_inf)
            s_ref[g] = s
            mx_ref[g:g + 1, :] = jnp.max(s, axis=0, keepdims=True)

    def stage_values(group, acc, alpha, top=top_block, weight=None):
        pv = None
        for g in range(ATTN_GROUP):
            b = jnp.clip(block_of(group, g, top), 0, n_blocks - 1)
            kb0 = pl.multiple_of(b * MOBA_BLOCK, MOBA_BLOCK)
            d = jnp.dot(vt_ref[:, pl.ds(kb0, MOBA_BLOCK)], p_ref[g], preferred_element_type=F32)
            pv = d if pv is None else pv + d
        if weight is not None:
            pv = pv * weight
        return alpha * acc + pv

    prev_top = top_block - ATTN_TILE_BLOCKS * jnp.where(step == n_tiles, 0, 1)
    prev_acc = stage_values(groups_of(prev_top) - 1, acc_ref[...], alpha_ref[...], prev_top)
    o_ref[...] = (prev_acc / l_ref[...]).T.astype(o_ref.dtype)

    stage_scores(0, causal=True)

    gate = lax.dot_general(kmean_ref[...].astype(BF16), q, nt, preferred_element_type=F32)
    blk = lax.broadcasted_iota(jnp.int32, gate.shape, 0)
    q_lane = lax.broadcasted_iota(jnp.int32, gate.shape, 1)
    own = tile * ATTN_TILE_BLOCKS + sum(
        (q_lane >= j * MOBA_BLOCK).astype(jnp.int32) for j in range(1, ATTN_TILE_BLOCKS))
    blk_f = blk.astype(F32)
    gate = jnp.where(blk < own, gate, neg_inf)
    selected = blk == own
    for j in range(min(MOBA_TOPK, n_blocks)):
        top = jnp.max(gate, axis=0, keepdims=True)
        first = jnp.min(jnp.where(gate == top, blk_f, float(n_blocks)), axis=0, keepdims=True)
        hit = blk_f == first
        selected = selected | (hit & (own > j))
        gate = jnp.where(hit, neg_inf, gate)
    bias_ref[0:n_blocks, :] = jnp.where(selected, 0.0, neg_inf)
    bias_ref[n_blocks:, :] = jnp.full((bias_ref.shape[0] - n_blocks, n_q), neg_inf, F32)

    def bias_row(group, g):
        b = block_of(group, g)
        return bias_ref[pl.ds(jnp.where(b < 0, n_blocks, b), 1), :]

    def trip(i, carry):
        m, l, acc, alpha_prev = carry
        acc = stage_values(i - 1, acc, alpha_prev, weight=jnp.where(i > 0, 1.0, 0.0))

        rows = [bias_row(i, g) for g in range(ATTN_GROUP)]
        m_new = m
        for g in range(ATTN_GROUP):
            m_new = jnp.maximum(m_new, mx_ref[g:g + 1, :] + rows[g])
        alpha = jnp.exp2(m - m_new)
        l = alpha * l
        for g in range(ATTN_GROUP):
            p = jnp.exp2(s_ref[g] - (m_new - rows[g]))
            l = l + jnp.sum(p, axis=0, keepdims=True)
            p_ref[g] = p.astype(BF16)

        stage_scores(i + 1, causal=False)
        return m_new, l, acc, alpha

    n_groups = jnp.where(step == n_tiles, 0, groups_of(top_block))
    init = (jnp.full((1, n_q), neg_inf, F32), jnp.zeros((1, n_q), F32),
            jnp.zeros((HEAD_DIM, n_q), F32), jnp.ones((1, n_q), F32))
    m, l, acc, alpha = lax.fori_loop(0, n_groups, trip, init)
    acc_ref[...] = acc
    l_ref[...] = jnp.where(l > 0.0, l, 1.0)
    alpha_ref[...] = alpha


def _attention(q, k, vt):
    seq, width = q.shape
    n_heads = width // HEAD_DIM
    n_blocks = seq // MOBA_BLOCK
    n_q = ATTN_TILE_BLOCKS * MOBA_BLOCK
    bias_rows = n_blocks + SUBLANES
    n_tiles = seq // n_q
    return pl.pallas_call(
        _attn_body,
        out_shape=jax.ShapeDtypeStruct((seq, width), BF16),
        grid=(n_heads, n_tiles + 1),
        in_specs=[
            pl.BlockSpec((n_q, HEAD_DIM), lambda hd, t: (jnp.minimum(t, n_tiles - 1), hd)),
            pl.BlockSpec((seq, HEAD_DIM), lambda hd, t: (0, hd)),
            pl.BlockSpec((HEAD_DIM, seq), lambda hd, t: (hd, 0)),
        ],
        out_specs=pl.BlockSpec((n_q, HEAD_DIM), lambda hd, t: (jnp.maximum(t - 1, 0), hd)),
        scratch_shapes=[pltpu.VMEM((n_blocks, HEAD_DIM), F32),
                        pltpu.VMEM((bias_rows, n_q), F32),
                        pltpu.VMEM((ATTN_GROUP, MOBA_BLOCK, n_q), F32),
                        pltpu.VMEM((ATTN_GROUP, n_q), F32),
                        pltpu.VMEM((ATTN_GROUP, MOBA_BLOCK, n_q), BF16),
                        pltpu.VMEM((HEAD_DIM, n_q), F32),
                        pltpu.VMEM((1, n_q), F32),
                        pltpu.VMEM((1, n_q), F32)],
        compiler_params=_params("arbitrary", "arbitrary"),
        name="moba_attn",
    )(q, k, vt)


def _moba_mixer(h, gain, w_qkv, q_gain, k_gain, w_out):
    q, k, vt = _qkv(h, gain, w_qkv, q_gain, k_gain)
    o = _attention(q, k, vt)
    return _proj_res(h, o, w_out)


def kernel(x, ffn_norm, ffn_w_gate, ffn_w_up, ffn_w_down, mix_norm, s5_w_in, s5_a_re, s5_a_im, s5_log_dt, s5_b_re, s5_b_im, s5_c_re, s5_c_im, s5_d, s5_w_glu, s5_w_out, moba_w_qkv, moba_q_norm, moba_k_norm, moba_w_out):
    bsz, seq, d = x.shape
    depth = ffn_norm.shape[0]
    d_ff = ffn_w_gate.shape[-1]
    wg_all = ffn_w_gate.reshape(depth * 2, d, d_ff).astype(BF16)
    wu_all = ffn_w_up.reshape(depth * 2, d, d_ff).astype(BF16)
    wd_all = ffn_w_down.reshape(depth * 2, d_ff, d).astype(BF16)
    outs = []
    for b in range(bsz):
        h = (x.reshape(seq, d) if bsz == 1 else x[b]).astype(F32)
        for layer in range(depth):
            h = _ffn(h, ffn_norm[layer, 0], wg_all, wu_all, wd_all, 2 * layer)
            i = layer // 2
            if layer % 2 == 0:
                h = _s5_mixer(h, mix_norm[layer], s5_w_in[i], s5_a_re[i], s5_a_im[i],
                              s5_log_dt[i], s5_b_re[i], s5_b_im[i], s5_c_re[i], s5_c_im[i],
                              s5_d[i], s5_w_glu[i], s5_w_out[i])
            else:
                h = _moba_mixer(h, mix_norm[layer], moba_w_qkv[i], moba_q_norm[i],
                                moba_k_norm[i], moba_w_out[i])
            h = _ffn(h, ffn_norm[layer, 1], wg_all, wu_all, wd_all, 2 * layer + 1)
        outs.append(h)
    out = outs[0].reshape(1, seq, d) if bsz == 1 else jnp.stack(outs)
    return out.astype(x.dtype)
```

```python
import functools

import jax
import jax.numpy as jnp
from jax import lax
from jax.experimental import pallas as pl
from jax.experimental.pallas import tpu as pltpu

F32 = jnp.float32
BF16 = jnp.bfloat16

LANES = 128
SUBLANES = 8
VMEM_LIMIT_BYTES = 56 * 1024 * 1024

EPS = 1e-6
MACARON_WEIGHT = 0.5
S5_GROUP = 16
S5_STATE = 64
GROUPS_PER_SLAB = LANES // S5_GROUP
SLAB_STATES = GROUPS_PER_SLAB * S5_STATE
N_HEADS = 8
HEAD_DIM = 128
MOBA_BLOCK = 256
MOBA_TOPK = 3
ROPE_THETA = 500000.0
ROT_DIM = HEAD_DIM // 4
ROT_HALF = ROT_DIM // 2

FFN_ROWS = 512
FFN_CHUNK = 256
PROJ_ROWS = 512
SSM_ROWS = 2048
ATTN_TILE_BLOCKS = 2
ATTN_GROUP = 8
LOG2_E = 1.4426950408889634


def _params(*semantics):
    return pltpu.CompilerParams(dimension_semantics=semantics,
                                vmem_limit_bytes=VMEM_LIMIT_BYTES)


def _rms_normed(x, gain):
    ms = jnp.mean(x * x, axis=-1, keepdims=True)
    return x * lax.rsqrt(ms + EPS) * gain


def _ffn_body(*refs, has_pre, has_post):
    refs = list(refs)
    x_ref, g_ref, wg_ref, wu_ref, wd_ref = refs[:5]
    rest = refs[5:]
    if has_pre:
        a_ref, wpre_ref = rest[:2]
        rest = rest[2:]
    if has_post:
        gpost_ref, wpost_ref = rest[:2]
        rest = rest[2:]
    o_ref = rest[0]
    act_ref = rest[-1]

    x = x_ref[...]
    if has_pre:
        x = x + jnp.dot(a_ref[...], wpre_ref[...], preferred_element_type=F32)
    xn = _rms_normed(x, g_ref[...]).astype(BF16)
    d_ff = wg_ref.shape[1]
    for j in range(d_ff // FFN_CHUNK):
        cols = slice(j * FFN_CHUNK, (j + 1) * FFN_CHUNK)
        gate = jnp.dot(xn, wg_ref[:, cols].astype(BF16), preferred_element_type=F32)
        up = jnp.dot(xn, wu_ref[:, cols].astype(BF16), preferred_element_type=F32)
        act_ref[:, cols] = (gate * jax.nn.sigmoid(gate) * up).astype(BF16)
    y = jnp.dot(act_ref[...], wd_ref[...].astype(BF16), preferred_element_type=F32)
    out = x + MACARON_WEIGHT * y
    o_ref[...] = out
    if has_post:
        nxt_ref = rest[1]
        outn = _rms_normed(out, gpost_ref[...]).astype(BF16)
        nxt_ref[...] = jnp.dot(outn, wpost_ref[...], preferred_element_type=F32)


def _ffn(h, gain, w_gate, w_up, w_down, idx, pre=None, post=None):
    seq, d = h.shape
    d_ff = w_gate.shape[2]
    rows = lambda n: pl.BlockSpec((FFN_ROWS, n), lambda i: (i, 0))
    once = lambda shape: pl.BlockSpec(shape, lambda i: (0,) * len(shape),
                                      pipeline_mode=pl.Buffered(1))
    stacked = lambda a, b: pl.BlockSpec((None, a, b), lambda i: (idx, 0, 0),
                                        pipeline_mode=pl.Buffered(1))
    args = [h, gain.reshape(1, d), w_gate, w_up, w_down]
    in_specs = [rows(d), once((1, d)), stacked(d, d_ff), stacked(d, d_ff), stacked(d_ff, d)]
    if pre is not None:
        a, w_pre = pre
        args += [a, w_pre.astype(BF16)]
        in_specs += [rows(a.shape[1]), once(w_pre.shape)]
    out_shape = jax.ShapeDtypeStruct((seq, d), F32)
    out_specs = rows(d)
    if post is not None:
        g_post, w_post = post
        args += [g_post.reshape(1, d), w_post.astype(BF16)]
        in_specs += [once((1, d)), once(w_post.shape)]
        out_shape = (out_shape, jax.ShapeDtypeStruct((seq, w_post.shape[1]), F32))
        out_specs = (out_specs, rows(w_post.shape[1]))
    return pl.pallas_call(
        functools.partial(_ffn_body, has_pre=pre is not None, has_post=post is not None),
        out_shape=out_shape,
        grid=(seq // FFN_ROWS,),
        in_specs=in_specs,
        out_specs=out_specs,
        scratch_shapes=[pltpu.VMEM((FFN_ROWS, d_ff), BF16)],
        compiler_params=_params("arbitrary"),
        name="ffn",
    )(*args)


def _s5_prep_body(are_ref, aim_ref, ldt_ref, bre_ref, bim_ref,
                  pre_ref, pim_ref, bbre_ref, bbim_ref):
    a_re = are_ref[...]
    a_im = aim_ref[...]
    dt = jnp.exp(ldt_ref[...])
    mag = jnp.exp(a_re * dt)
    lam_re = mag * jnp.cos(a_im * dt)
    lam_im = mag * jnp.sin(a_im * dt)
    den = a_re * a_re + a_im * a_im
    num_re = lam_re - 1.0
    coef_re = (num_re * a_re + lam_im * a_im) / den
    coef_im = (lam_im * a_re - num_re * a_im) / den
    b_re = bre_ref[...]
    b_im = bim_ref[...]
    bbre_ref[...] = coef_re * b_re - coef_im * b_im
    bbim_ref[...] = coef_re * b_im + coef_im * b_re
    p_re, p_im = lam_re, lam_im
    for r in range(SUBLANES):
        pre_ref[r:r + 1, :] = p_re
        pim_ref[r:r + 1, :] = p_im
        p_re, p_im = p_re * lam_re - p_im * lam_im, p_re * lam_im + p_im * lam_re


def _s5_prep(a_re, a_im, log_dt, b_re, b_im):
    groups, states = a_re.shape
    n_flat = groups * states
    row = lambda a: a.astype(F32).reshape(1, n_flat)
    ldt = jnp.broadcast_to(log_dt.astype(F32)[:, None], (groups, states)).reshape(1, n_flat)
    chan_rows = lambda b: b.astype(F32).transpose(2, 0, 1).reshape(S5_GROUP, n_flat)
    return pl.pallas_call(
        _s5_prep_body,
        out_shape=(
            jax.ShapeDtypeStruct((SUBLANES, n_flat), F32),
            jax.ShapeDtypeStruct((SUBLANES, n_flat), F32),
            jax.ShapeDtypeStruct((S5_GROUP, n_flat), F32),
            jax.ShapeDtypeStruct((S5_GROUP, n_flat), F32),
        ),
        name="s5_prep",
    )(row(a_re), row(a_im), ldt, chan_rows(b_re), chan_rows(b_im))


def _s5_ops_body(wb_ref, wc_ref, wct_ref, pre_ref, pim_ref,
                 kcat_ref, wg_ref, wet_ref, p8re_ref, p8im_ref):
    ns = SLAB_STATES
    b_re, b_im = wb_ref[0, :, 0:ns], wb_ref[0, :, ns:2 * ns]
    c_re, c_im = wc_ref[0, 0:ns, :], wc_ref[0, ns:2 * ns, :]
    ct_re, ct_im = wct_ref[0, :, 0:ns], wct_ref[0, :, ns:2 * ns]
    powers = [(jnp.ones((1, ns), F32), jnp.zeros((1, ns), F32))]
    powers += [(pre_ref[0, r:r + 1, :], pim_ref[0, r:r + 1, :]) for r in range(SUBLANES)]
    hi = lax.Precision.HIGHEST
    for t in range(SUBLANES):
        l_re, l_im = powers[t]
        lb_re = b_re * l_re - b_im * l_im
        lb_im = b_re * l_im + b_im * l_re
        k_t = (jnp.dot(lb_re, c_re, precision=hi, preferred_element_type=F32)
               - jnp.dot(lb_im, c_im, precision=hi, preferred_element_type=F32))
        kcat_ref[0, t * LANES:(t + 1) * LANES, :] = k_t.astype(kcat_ref.dtype)
        k = SUBLANES - 1 - t
        wg_ref[0, k * LANES:(k + 1) * LANES, 0:ns] = lb_re.astype(wg_ref.dtype)
        wg_ref[0, k * LANES:(k + 1) * LANES, ns:2 * ns] = lb_im.astype(wg_ref.dtype)
    for k in range(SUBLANES):
        l_re, l_im = powers[k + 1]
        wet_ref[0, k * LANES:(k + 1) * LANES, 0:ns] = (ct_re * l_re - ct_im * l_im).astype(wet_ref.dtype)
        wet_ref[0, k * LANES:(k + 1) * LANES, ns:2 * ns] = (-(ct_re * l_im + ct_im * l_re)).astype(wet_ref.dtype)
    m_re, m_im = powers[SUBLANES]
    q_re, q_im = m_re, m_im
    for r in range(SUBLANES):
        p8re_ref[0, r:r + 1, :] = q_re
        p8im_ref[0, r:r + 1, :] = q_im
        q_re, q_im = q_re * m_re - q_im * m_im, q_re * m_im + q_im * m_re


def _s5_ops(wb, wc, wct, pow_re, pow_im):
    n_slabs = wb.shape[0]
    ns = SLAB_STATES
    wide = SUBLANES * LANES
    slab = lambda *shape: pl.BlockSpec((1,) + shape, lambda j: (j, 0, 0))
    return pl.pallas_call(
        _s5_ops_body,
        out_shape=(
            jax.ShapeDtypeStruct((n_slabs, wide, LANES), BF16),
            jax.ShapeDtypeStruct((n_slabs, wide, 2 * ns), BF16),
            jax.ShapeDtypeStruct((n_slabs, wide, 2 * ns), BF16),
            jax.ShapeDtypeStruct((n_slabs, SUBLANES, ns), F32),
            jax.ShapeDtypeStruct((n_slabs, SUBLANES, ns), F32),
        ),
        grid=(n_slabs,),
        in_specs=[slab(LANES, 2 * ns), slab(2 * ns, LANES), slab(LANES, 2 * ns),
                  slab(SUBLANES, ns), slab(SUBLANES, ns)],
        out_specs=(slab(wide, LANES), slab(wide, 2 * ns), slab(wide, 2 * ns),
                   slab(SUBLANES, ns), slab(SUBLANES, ns)),
        compiler_params=_params("arbitrary"),
        name="s5_ops",
    )(wb, wc, wct, pow_re, pow_im)


def _s5_ssm_body(u_ref, uold_ref, kcat_ref, wg_ref, we_ref, pre_ref, pim_ref, y_ref,
                 ucat_ref, g_ref, h_ref, carry_ref):
    t_idx = pl.program_id(1)
    ns = SLAB_STATES
    rows = u_ref.shape[0]
    n_chunks = rows // SUBLANES
    slot = t_idx % 2

    @pl.when(t_idx == 0)
    def _():
        carry_ref[...] = jnp.zeros_like(carry_ref)
        g_ref[...] = jnp.zeros_like(g_ref)
        h_ref[...] = jnp.zeros_like(h_ref)

    u3 = uold_ref[...].reshape(n_chunks, SUBLANES, LANES)
    row = lax.broadcasted_iota(jnp.int32, u3.shape, 1)
    for tau in range(SUBLANES):
        shifted = u3 if tau == 0 else jnp.where(row >= tau, pltpu.roll(u3, tau, 1), 0.0)
        ucat_ref[:, tau * LANES:(tau + 1) * LANES] = shifted.reshape(rows, LANES).astype(BF16)
    y_ref[...] = jnp.dot(ucat_ref[...], kcat_ref[0], preferred_element_type=F32)
    h_in = h_ref[slot, pl.ds(SUBLANES - 1, n_chunks), :].astype(BF16)
    y2 = jnp.dot(h_in, we_ref[0], preferred_element_type=F32)
    for k in range(SUBLANES):
        y_ref[pl.ds(k, n_chunks, stride=SUBLANES), :] += y2[:, k * LANES:(k + 1) * LANES]

    p_re = pre_ref[0]
    p_im = pim_ref[0]
    srow = lax.broadcasted_iota(jnp.int32, (SUBLANES, ns), 0)
    steps = []
    for s in (1, 2, 4):
        keep = srow >= s
        steps.append((s,
                      jnp.where(keep, p_re[s - 1:s, :], 0.0),
                      jnp.where(keep, p_im[s - 1:s, :], 0.0)))
    g_old = g_ref.at[1 - slot]
    h_new = h_ref.at[1 - slot]
    h_new[0:SUBLANES, :] = carry_ref[...]
    c_re, c_im = carry_ref[:, 0:ns], carry_ref[:, ns:2 * ns]
    last = SUBLANES - 1
    for i in range(n_chunks // SUBLANES):
        r0 = i * SUBLANES
        x_re = g_old[r0:r0 + SUBLANES, 0:ns]
        x_im = g_old[r0:r0 + SUBLANES, ns:2 * ns]
        for s, a_re, a_im in steps:
            s_re = pltpu.roll(x_re, s, 0)
            s_im = pltpu.roll(x_im, s, 0)
            x_re, x_im = (x_re + (a_re * s_re - a_im * s_im),
                          x_im + (a_re * s_im + a_im * s_re))
        x_re, x_im = (x_re + (p_re * c_re - p_im * c_im),
                      x_im + (p_re * c_im + p_im * c_re))
        h_new[r0 + SUBLANES:r0 + 2 * SUBLANES, 0:ns] = x_re
        h_new[r0 + SUBLANES:r0 + 2 * SUBLANES, ns:2 * ns] = x_im
        c_re = jnp.broadcast_to(x_re[last:last + 1, :], (SUBLANES, ns))
        c_im = jnp.broadcast_to(x_im[last:last + 1, :], (SUBLANES, ns))
    carry_ref[:, 0:ns] = c_re
    carry_ref[:, ns:2 * ns] = c_im

    per_step = [u_ref[pl.ds(k, n_chunks, stride=SUBLANES), :].astype(BF16)
                for k in range(SUBLANES)]
    g_ref[slot] = jnp.dot(jnp.concatenate(per_step, axis=1), wg_ref[0],
                          preferred_element_type=F32)


def _s5_ssm(u, kcat, wg, we, pow8_re, pow8_im):
    seq, width = u.shape
    n_slabs = width // LANES
    ns = SLAB_STATES
    wide = SUBLANES * LANES
    rows = min(SSM_ROWS, seq)
    n_chunks = rows // SUBLANES
    n_tiles = seq // rows
    slab = lambda *shape: pl.BlockSpec((1,) + shape, lambda b, t: (b, 0, 0))
    return pl.pallas_call(
        _s5_ssm_body,
        out_shape=jax.ShapeDtypeStruct((seq, width), F32),
        grid=(n_slabs, n_tiles + 2),
        in_specs=[
            pl.BlockSpec((rows, LANES), lambda b, t: (jnp.minimum(t, n_tiles - 1), b)),
            pl.BlockSpec((rows, LANES), lambda b, t: (jnp.maximum(t - 2, 0), b)),
            slab(wide, LANES), slab(wide, 2 * ns), slab(2 * ns, wide),
            slab(SUBLANES, ns), slab(SUBLANES, ns),
        ],
        out_specs=pl.BlockSpec((rows, LANES), lambda b, t: (jnp.maximum(t - 2, 0), b)),
        scratch_shapes=[
            pltpu.VMEM((rows, wide), BF16),
            pltpu.VMEM((2, n_chunks, 2 * ns), F32),
            pltpu.VMEM((2, n_chunks + SUBLANES, 2 * ns), F32),
            pltpu.VMEM((SUBLANES, 2 * ns), F32),
        ],
        compiler_params=_params("arbitrary", "arbitrary"),
        name="s5_ssm",
    )(u, u, kcat, wg, we, pow8_re, pow8_im)


def _slab_block_diag(w):
    groups, a, b = w.shape
    n_slabs = groups // GROUPS_PER_SLAB
    eye = jnp.eye(GROUPS_PER_SLAB, dtype=w.dtype)
    w4 = w.reshape(n_slabs, GROUPS_PER_SLAB, a, b)
    out = jnp.einsum('jgab,gh->jgahb', w4, eye)
    return out.reshape(n_slabs, GROUPS_PER_SLAB * a, GROUPS_PER_SLAB * b)


def _s5_out_body(h_ref, y_ref, u_ref, d_ref, wglu_ref, wout_ref, o_ref):
    g = jax.nn.gelu(y_ref[...] + d_ref[...] * u_ref[...])
    gate = jnp.dot(g.astype(BF16), wglu_ref[...], preferred_element_type=F32)
    g = g * jax.nn.sigmoid(gate)
    o_ref[...] = h_ref[...] + jnp.dot(g.astype(BF16), wout_ref[...], preferred_element_type=F32)


def _s5_out(h, y, u, d_skip, w_glu, w_out):
    seq, d = h.shape
    width = u.shape[1]
    row_spec = lambda n: pl.BlockSpec((PROJ_ROWS, n), lambda i: (i, 0))
    return pl.pallas_call(
        _s5_out_body,
        out_shape=jax.ShapeDtypeStruct((seq, d), F32),
        grid=(seq // PROJ_ROWS,),
        in_specs=[
            row_spec(d), row_spec(width), row_spec(width),
            pl.BlockSpec((1, width), lambda i: (0, 0)),
            pl.BlockSpec((width, width), lambda i: (0, 0)),
            pl.BlockSpec((width, d), lambda i: (0, 0)),
        ],
        out_specs=row_spec(d),
        compiler_params=_params("arbitrary"),
        name="s5_out",
    )(h, y, u, d_skip.astype(F32).reshape(1, width), w_glu.astype(BF16), w_out.astype(BF16))


def _s5_mixer(h, u, a_re, a_im, log_dt, b_re, b_im, c_re, c_im, d_skip, w_glu, w_out):
    groups, states = a_re.shape
    pow_re, pow_im, bb_re, bb_im = _s5_prep(a_re, a_im, log_dt, b_re, b_im)
    n_slabs = groups // GROUPS_PER_SLAB
    to_slabs = lambda p: p.reshape(SUBLANES, n_slabs, SLAB_STATES).transpose(1, 0, 2)
    per_group_b = lambda b: b.reshape(S5_GROUP, groups, states).transpose(1, 0, 2)
    wb = jnp.concatenate([_slab_block_diag(per_group_b(bb_re)),
                          _slab_block_diag(per_group_b(bb_im))], axis=-1)
    wct = jnp.concatenate([_slab_block_diag(c_re.astype(F32)),
                           _slab_block_diag(c_im.astype(F32))], axis=-1)
    wc = wct.transpose(0, 2, 1)
    kcat, wg, wet, pow8_re, pow8_im = _s5_ops(wb, wc, wct, to_slabs(pow_re), to_slabs(pow_im))
    y = _s5_ssm(u, kcat, wg, wet.transpose(0, 2, 1), pow8_re, pow8_im)
    return _s5_out(h, y, u, d_skip, w_glu, w_out)


def _head_dim_order():
    idx = list(range(ROT_HALF)) + list(range(ROT_DIM, HEAD_DIM // 2 + ROT_HALF))
    idx += list(range(ROT_HALF, ROT_DIM)) + list(range(HEAD_DIM // 2 + ROT_HALF, HEAD_DIM))
    return jnp.asarray(idx, jnp.int32)


def _rope(x, cos, sin_signed):
    return x * cos + pltpu.roll(x, HEAD_DIM // 2, 1) * sin_signed


def _qkv_body(x_ref, g_ref, w_ref, wvt_ref, qg_ref, kg_ref, invf_ref, q_ref, k_ref, vt_ref,
              cos_rel_ref, sin_rel_ref, y_ref):
    step = pl.program_id(0)
    rows = x_ref.shape[0]
    width = q_ref.shape[1]

    @pl.when(step == 0)
    def _():
        rel = lax.broadcasted_iota(jnp.int32, (rows, HEAD_DIM), 0).astype(F32) * invf_ref[...]
        cos_rel_ref[...] = jnp.cos(rel)
        sin_rel_ref[...] = jnp.sin(rel)
        y_ref[...] = jnp.zeros_like(y_ref)

    base = (jnp.maximum(step - 1, 0) * rows).astype(F32) * invf_ref[...]
    cos0, sin0 = jnp.cos(base), jnp.sin(base)
    cos = cos0 * cos_rel_ref[...] - sin0 * sin_rel_ref[...]
    sin = sin0 * cos_rel_ref[...] + cos0 * sin_rel_ref[...]
    lane = lax.broadcasted_iota(jnp.int32, (rows, HEAD_DIM), 1)
    half = HEAD_DIM // 2
    sin_signed = jnp.where(lane < ROT_HALF, -sin,
                           jnp.where((lane >= half) & (lane < half + ROT_HALF), sin, 0.0))

    def finish(part, gain, scale, out_ref):
        for hd in range(width // HEAD_DIM):
            cols = slice(hd * HEAD_DIM, (hd + 1) * HEAD_DIM)
            yh = y_ref[:, part * width + hd * HEAD_DIM:part * width + (hd + 1) * HEAD_DIM]
            yh = _rope(_rms_normed(yh, gain), cos, sin_signed)
            if scale is not None:
                yh = yh * scale
            out_ref[:, cols] = yh.astype(out_ref.dtype)

    finish(0, qg_ref[...], HEAD_DIM ** -0.5 * LOG2_E, q_ref)
    finish(1, kg_ref[...], None, k_ref)

    xn = _rms_normed(x_ref[...], g_ref[...]).astype(BF16)
    vt = lax.dot_general(wvt_ref[...], xn, (((1,), (1,)), ((), ())), preferred_element_type=F32)
    vt_ref[...] = vt.astype(vt_ref.dtype)
    y_ref[...] = jnp.dot(xn, w_ref[...], preferred_element_type=F32)


def _qkv(h, gain, w_qkv, q_gain, k_gain):
    seq, d = h.shape
    width = w_qkv.shape[1] // 3
    inv_freq = ROPE_THETA ** (-jnp.arange(0, ROT_DIM, 2, dtype=F32) / ROT_DIM)
    gap = jnp.zeros((HEAD_DIM // 2 - ROT_HALF,), F32)
    invf = jnp.concatenate([inv_freq, gap, inv_freq, gap]).reshape(1, HEAD_DIM)
    out = jax.ShapeDtypeStruct((seq, width), BF16)
    n_tiles = seq // PROJ_ROWS
    cur_rows = lambda n: pl.BlockSpec((PROJ_ROWS, n), lambda i: (jnp.minimum(i, n_tiles - 1), 0))
    prev_rows = lambda n: pl.BlockSpec((PROJ_ROWS, n), lambda i: (jnp.maximum(i - 1, 0), 0))
    const = lambda shape: pl.BlockSpec(shape, lambda i: (0, 0))
    order = _head_dim_order()
    cols = (jnp.arange(2 * width // HEAD_DIM, dtype=jnp.int32)[:, None] * HEAD_DIM
            + order[None, :]).reshape(-1)
    w_qk = w_qkv[:, :2 * width][:, cols].astype(BF16)
    w_vt = w_qkv[:, 2 * width:].T.astype(BF16)
    q_gain, k_gain = q_gain[order], k_gain[order]
    return pl.pallas_call(
        _qkv_body,
        out_shape=(out, out, jax.ShapeDtypeStruct((width, seq), BF16)),
        grid=(n_tiles + 1,),
        in_specs=[cur_rows(d), const((1, d)), const((d, 2 * width)), const((width, d)),
                  const((1, HEAD_DIM)), const((1, HEAD_DIM)), const((1, HEAD_DIM))],
        out_specs=(prev_rows(width), prev_rows(width),
                   pl.BlockSpec((width, PROJ_ROWS), lambda i: (0, jnp.minimum(i, n_tiles - 1)))),
        scratch_shapes=[pltpu.VMEM((PROJ_ROWS, HEAD_DIM), F32),
                        pltpu.VMEM((PROJ_ROWS, HEAD_DIM), F32),
                        pltpu.VMEM((PROJ_ROWS, 2 * width), F32)],
        compiler_params=_params("arbitrary"),
        name="moba_qkv",
    )(h, gain.reshape(1, d), w_qk, w_vt,
      q_gain.astype(F32).reshape(1, HEAD_DIM), k_gain.astype(F32).reshape(1, HEAD_DIM), invf)


def _attn_body(q_ref, k_ref, vt_ref, o_ref, kmean_ref, bias_ref, s_ref, mx_ref, p_ref,
               acc_ref, l_ref, alpha_ref):
    step = pl.program_id(1)
    n_tiles = pl.num_programs(1) - 1
    tile = jnp.minimum(step, n_tiles - 1)
    n_blocks = k_ref.shape[0] // MOBA_BLOCK
    n_q = q_ref.shape[0]
    top_block = tile * ATTN_TILE_BLOCKS + (ATTN_TILE_BLOCKS - 1)
    neg_inf = -jnp.inf

    @pl.when((pl.program_id(0) == 0) & (step == 0))
    def _():
        p_ref[...] = jnp.zeros(p_ref.shape, BF16)
        acc_ref[...] = jnp.zeros(acc_ref.shape, F32)
        l_ref[...] = jnp.ones(l_ref.shape, F32)
        alpha_ref[...] = jnp.ones(alpha_ref.shape, F32)

    @pl.when(step == 0)
    def _():
        for b in range(n_blocks):
            kb = k_ref[b * MOBA_BLOCK:(b + 1) * MOBA_BLOCK, :].astype(F32)
            kmean_ref[b:b + 1, :] = jnp.mean(kb, axis=0, keepdims=True)

    q = q_ref[...]
    nt = (((1,), (1,)), ((), ()))

    def groups_of(top):
        return (top + ATTN_GROUP) // ATTN_GROUP

    def block_of(group, g, top=top_block):
        return top - (group * ATTN_GROUP + g)

    def scores(block):
        kb0 = pl.multiple_of(jnp.maximum(block, 0) * MOBA_BLOCK, MOBA_BLOCK)
        return lax.dot_general(k_ref[pl.ds(kb0, MOBA_BLOCK), :], q, nt,
                               preferred_element_type=F32)

    def stage_scores(group, causal):
        for g in range(ATTN_GROUP):
            s = scores(block_of(group, g))
            if causal and g < ATTN_TILE_BLOCKS:
                k_pos = lax.broadcasted_iota(jnp.int32, s.shape, 0)
                q_pos = lax.broadcasted_iota(jnp.int32, s.shape, 1)
                q_rel = q_pos - (ATTN_TILE_BLOCKS - 1 - g) * MOBA_BLOCK
                s = jnp.where((q_rel >= MOBA_BLOCK) | (k_pos <= q_rel), s, neg_inf)
            s_ref[g] = s
            mx_ref[g:g + 1, :] = jnp.max(s, axis=0, keepdims=True)

    def stage_values(group, acc, alpha, top=top_block, weight=None):
        pv = None
        for g in range(ATTN_GROUP):
            b = jnp.clip(block_of(group, g, top), 0, n_blocks - 1)
            kb0 = pl.multiple_of(b * MOBA_BLOCK, MOBA_BLOCK)
            d = jnp.dot(vt_ref[:, pl.ds(kb0, MOBA_BLOCK)], p_ref[g], preferred_element_type=F32)
            pv = d if pv is None else pv + d
        if weight is not None:
            pv = pv * weight
        return alpha * acc + pv

    prev_top = top_block - ATTN_TILE_BLOCKS * jnp.where(step == n_tiles, 0, 1)
    prev_acc = stage_values(groups_of(prev_top) - 1, acc_ref[...], alpha_ref[...], prev_top)
    o_ref[...] = (prev_acc / l_ref[...]).T.astype(o_ref.dtype)

    stage_scores(0, causal=True)

    gate = lax.dot_general(kmean_ref[...].astype(BF16), q, nt, preferred_element_type=F32)
    blk = lax.broadcasted_iota(jnp.int32, gate.shape, 0)
    q_lane = lax.broadcasted_iota(jnp.int32, gate.shape, 1)
    own = tile * ATTN_TILE_BLOCKS + sum(
        (q_lane >= j * MOBA_BLOCK).astype(jnp.int32) for j in range(1, ATTN_TILE_BLOCKS))
    blk_f = blk.astype(F32)
    gate = jnp.where(blk < own, gate, neg_inf)
    selected = blk == own
    for j in range(min(MOBA_TOPK, n_blocks)):
        top = jnp.max(gate, axis=0, keepdims=True)
        first = jnp.min(jnp.where(gate == top, blk_f, float(n_blocks)), axis=0, keepdims=True)
        hit = blk_f == first
        selected = selected | (hit & (own > j))
        gate = jnp.where(hit, neg_inf, gate)
    bias_ref[0:n_blocks, :] = jnp.where(selected, 0.0, neg_inf)
    bias_ref[n_blocks:, :] = jnp.full((bias_ref.shape[0] - n_blocks, n_q), neg_inf, F32)

    def bias_row(group, g):
        b = block_of(group, g)
        return bias_ref[pl.ds(jnp.where(b < 0, n_blocks, b), 1), :]

    def trip(i, carry):
        m, l, acc, alpha_prev = carry
        acc = stage_values(i - 1, acc, alpha_prev, weight=jnp.where(i > 0, 1.0, 0.0))

        rows = [bias_row(i, g) for g in range(ATTN_GROUP)]
        m_new = m
        for g in range(ATTN_GROUP):
            m_new = jnp.maximum(m_new, mx_ref[g:g + 1, :] + rows[g])
        alpha = jnp.exp2(m - m_new)
        l = alpha * l
        for g in range(ATTN_GROUP):
            p = jnp.exp2(s_ref[g] - (m_new - rows[g]))
            l = l + jnp.sum(p, axis=0, keepdims=True)
            p_ref[g] = p.astype(BF16)

        stage_scores(i + 1, causal=False)
        return m_new, l, acc, alpha

    n_groups = jnp.where(step == n_tiles, 0, groups_of(top_block))
    init = (jnp.full((1, n_q), neg_inf, F32), jnp.zeros((1, n_q), F32),
            jnp.zeros((HEAD_DIM, n_q), F32), jnp.ones((1, n_q), F32))
    m, l, acc, alpha = lax.fori_loop(0, n_groups, trip, init)
    acc_ref[...] = acc
    l_ref[...] = jnp.where(l > 0.0, l, 1.0)
    alpha_ref[...] = alpha


def _attention(q, k, vt):
    seq, width = q.shape
    n_heads = width // HEAD_DIM
    n_blocks = seq // MOBA_BLOCK
    n_q = ATTN_TILE_BLOCKS * MOBA_BLOCK
    bias_rows = n_blocks + SUBLANES
    n_tiles = seq // n_q
    return pl.pallas_call(
        _attn_body,
        out_shape=jax.ShapeDtypeStruct((seq, width), BF16),
        grid=(n_heads, n_tiles + 1),
        in_specs=[
            pl.BlockSpec((n_q, HEAD_DIM), lambda hd, t: (jnp.minimum(t, n_tiles - 1), hd)),
            pl.BlockSpec((seq, HEAD_DIM), lambda hd, t: (0, hd)),
            pl.BlockSpec((HEAD_DIM, seq), lambda hd, t: (hd, 0)),
        ],
        out_specs=pl.BlockSpec((n_q, HEAD_DIM), lambda hd, t: (jnp.maximum(t - 1, 0), hd)),
        scratch_shapes=[pltpu.VMEM((n_blocks, HEAD_DIM), F32),
                        pltpu.VMEM((bias_rows, n_q), F32),
                        pltpu.VMEM((ATTN_GROUP, MOBA_BLOCK, n_q), F32),
                        pltpu.VMEM((ATTN_GROUP, n_q), F32),
                        pltpu.VMEM((ATTN_GROUP, MOBA_BLOCK, n_q), BF16),
                        pltpu.VMEM((HEAD_DIM, n_q), F32),
                        pltpu.VMEM((1, n_q), F32),
                        pltpu.VMEM((1, n_q), F32)],
        compiler_params=_params("arbitrary", "arbitrary"),
        name="moba_attn",
    )(q, k, vt)


def _moba_heads(h, gain, w_qkv, q_gain, k_gain):
    q, k, vt = _qkv(h, gain, w_qkv, q_gain, k_gain)
    return _attention(q, k, vt)


def kernel(x, ffn_norm, ffn_w_gate, ffn_w_up, ffn_w_down, mix_norm, s5_w_in, s5_a_re, s5_a_im, s5_log_dt, s5_b_re, s5_b_im, s5_c_re, s5_c_im, s5_d, s5_w_glu, s5_w_out, moba_w_qkv, moba_q_norm, moba_k_norm, moba_w_out):
    bsz, seq, d = x.shape
    depth = ffn_norm.shape[0]
    d_ff = ffn_w_gate.shape[-1]
    wg_all = ffn_w_gate.reshape(depth * 2, d, d_ff)
    wu_all = ffn_w_up.reshape(depth * 2, d, d_ff)
    wd_all = ffn_w_down.reshape(depth * 2, d_ff, d)
    outs = []
    for b in range(bsz):
        h = (x.reshape(seq, d) if bsz == 1 else x[b]).astype(F32)
        for layer in range(depth):
            i = layer // 2
            ffn = functools.partial(_ffn, w_gate=wg_all, w_up=wu_all, w_down=wd_all)
            if layer % 2 == 0:
                h, u = ffn(h, ffn_norm[layer, 0], idx=2 * layer,
                           post=(mix_norm[layer], s5_w_in[i]))
                h = _s5_mixer(h, u, s5_a_re[i], s5_a_im[i], s5_log_dt[i], s5_b_re[i],
                              s5_b_im[i], s5_c_re[i], s5_c_im[i], s5_d[i], s5_w_glu[i],
                              s5_w_out[i])
                h = ffn(h, ffn_norm[layer, 1], idx=2 * layer + 1)
            else:
                h = ffn(h, ffn_norm[layer, 0], idx=2 * layer)
                o = _moba_heads(h, mix_norm[layer], moba_w_qkv[i], moba_q_norm[i],
                                moba_k_norm[i])
                h = ffn(h, ffn_norm[layer, 1], idx=2 * layer + 1, pre=(o, moba_w_out[i]))
        outs.append(h)
    out = outs[0].reshape(1, seq, d) if bsz == 1 else jnp.stack(outs)
    return out.astype(x.dtype)
```

```python
import functools

import jax
import jax.numpy as jnp
from jax import lax
from jax.experimental import pallas as pl
from jax.experimental.pallas import tpu as pltpu

F32 = jnp.float32
BF16 = jnp.bfloat16

LANES = 128
SUBLANES = 8
VMEM_LIMIT_BYTES = 56 * 1024 * 1024

EPS = 1e-6
MACARON_WEIGHT = 0.5
S5_GROUP = 16
S5_STATE = 64
GROUPS_PER_SLAB = LANES // S5_GROUP
SLAB_STATES = GROUPS_PER_SLAB * S5_STATE
N_HEADS = 8
HEAD_DIM = 128
MOBA_BLOCK = 256
MOBA_TOPK = 3
ROPE_THETA = 500000.0
ROT_DIM = HEAD_DIM // 4
ROT_HALF = ROT_DIM // 2

FFN_ROWS = 512
FFN_CHUNK = 256
PROJ_ROWS = 512
SSM_ROWS = 2048
ATTN_TILE_BLOCKS = 2
ATTN_GROUP = 8
LOG2_E = 1.4426950408889634


def _params(*semantics):
    return pltpu.CompilerParams(dimension_semantics=semantics,
                                vmem_limit_bytes=VMEM_LIMIT_BYTES)


def _rms_normed(x, gain):
    ms = jnp.mean(x * x, axis=-1, keepdims=True)
    return x * lax.rsqrt(ms + EPS) * gain


def _ffn_body(*refs, has_pre, has_post):
    refs = list(refs)
    x_ref, g_ref, wg_ref, wu_ref, wd_ref = refs[:5]
    rest = refs[5:]
    if has_pre:
        a_ref, wpre_ref = rest[:2]
        rest = rest[2:]
    if has_post:
        gpost_ref, wpost_ref = rest[:2]
        rest = rest[2:]
    o_ref = rest[0]
    act_ref = rest[-1]

    x = x_ref[...]
    if has_pre:
        x = x + jnp.dot(a_ref[...], wpre_ref[...], preferred_element_type=F32)
    xn = _rms_normed(x, g_ref[...]).astype(BF16)
    d_ff = wg_ref.shape[1]
    for j in range(d_ff // FFN_CHUNK):
        cols = slice(j * FFN_CHUNK, (j + 1) * FFN_CHUNK)
        gate = jnp.dot(xn, wg_ref[:, cols].astype(BF16), preferred_element_type=F32)
        up = jnp.dot(xn, wu_ref[:, cols].astype(BF16), preferred_element_type=F32)
        act_ref[:, cols] = (gate * jax.nn.sigmoid(gate) * up).astype(BF16)
    y = jnp.dot(act_ref[...], wd_ref[...].astype(BF16), preferred_element_type=F32)
    out = x + MACARON_WEIGHT * y
    o_ref[...] = out
    if has_post:
        nxt_ref = rest[1]
        outn = _rms_normed(out, gpost_ref[...]).astype(BF16)
        nxt = jnp.dot(outn, wpost_ref[...], preferred_element_type=F32)
        for s in range(nxt_ref.shape[0]):
            nxt_ref[s] = nxt[:, s * LANES:(s + 1) * LANES]


def _ffn(h, gain, w_gate, w_up, w_down, idx, pre=None, post=None):
    seq, d = h.shape
    d_ff = w_gate.shape[2]
    rows = lambda n: pl.BlockSpec((FFN_ROWS, n), lambda i: (i, 0))
    once = lambda shape: pl.BlockSpec(shape, lambda i: (0,) * len(shape),
                                      pipeline_mode=pl.Buffered(1))
    stacked = lambda a, b: pl.BlockSpec((None, a, b), lambda i: (idx, 0, 0),
                                        pipeline_mode=pl.Buffered(1))
    args = [h, gain.reshape(1, d), w_gate, w_up, w_down]
    in_specs = [rows(d), once((1, d)), stacked(d, d_ff), stacked(d, d_ff), stacked(d_ff, d)]
    if pre is not None:
        a, w_pre = pre
        args += [a, w_pre.astype(BF16)]
        in_specs += [rows(a.shape[1]), once(w_pre.shape)]
    out_shape = jax.ShapeDtypeStruct((seq, d), F32)
    out_specs = rows(d)
    if post is not None:
        g_post, w_post = post
        args += [g_post.reshape(1, d), w_post.astype(BF16)]
        in_specs += [once((1, d)), once(w_post.shape)]
        n_slabs = w_post.shape[1] // LANES
        out_shape = (out_shape, jax.ShapeDtypeStruct((n_slabs, seq, LANES), F32))
        out_specs = (out_specs, pl.BlockSpec((n_slabs, FFN_ROWS, LANES), lambda i: (0, i, 0)))
    return pl.pallas_call(
        functools.partial(_ffn_body, has_pre=pre is not None, has_post=post is not None),
        out_shape=out_shape,
        grid=(seq // FFN_ROWS,),
        in_specs=in_specs,
        out_specs=out_specs,
        scratch_shapes=[pltpu.VMEM((FFN_ROWS, d_ff), BF16)],
        compiler_params=_params("arbitrary"),
        name="ffn",
    )(*args)


def _s5_prep_body(are_ref, aim_ref, ldt_ref, bre_ref, bim_ref,
                  pre_ref, pim_ref, bbre_ref, bbim_ref):
    a_re = are_ref[...]
    a_im = aim_ref[...]
    dt = jnp.exp(ldt_ref[...])
    mag = jnp.exp(a_re * dt)
    lam_re = mag * jnp.cos(a_im * dt)
    lam_im = mag * jnp.sin(a_im * dt)
    den = a_re * a_re + a_im * a_im
    num_re = lam_re - 1.0
    coef_re = (num_re * a_re + lam_im * a_im) / den
    coef_im = (lam_im * a_re - num_re * a_im) / den
    b_re = bre_ref[...]
    b_im = bim_ref[...]
    bbre_ref[...] = coef_re * b_re - coef_im * b_im
    bbim_ref[...] = coef_re * b_im + coef_im * b_re
    p_re, p_im = lam_re, lam_im
    for r in range(SUBLANES):
        pre_ref[r:r + 1, :] = p_re
        pim_ref[r:r + 1, :] = p_im
        p_re, p_im = p_re * lam_re - p_im * lam_im, p_re * lam_im + p_im * lam_re


def _s5_prep(a_re, a_im, log_dt, b_re, b_im):
    groups, states = a_re.shape
    n_flat = groups * states
    row = lambda a: a.astype(F32).reshape(1, n_flat)
    ldt = jnp.broadcast_to(log_dt.astype(F32)[:, None], (groups, states)).reshape(1, n_flat)
    chan_rows = lambda b: b.astype(F32).transpose(2, 0, 1).reshape(S5_GROUP, n_flat)
    return pl.pallas_call(
        _s5_prep_body,
        out_shape=(
            jax.ShapeDtypeStruct((SUBLANES, n_flat), F32),
            jax.ShapeDtypeStruct((SUBLANES, n_flat), F32),
            jax.ShapeDtypeStruct((S5_GROUP, n_flat), F32),
            jax.ShapeDtypeStruct((S5_GROUP, n_flat), F32),
        ),
        name="s5_prep",
    )(row(a_re), row(a_im), ldt, chan_rows(b_re), chan_rows(b_im))


def _s5_ops_body(wb_ref, wc_ref, wct_ref, pre_ref, pim_ref,
                 kcat_ref, wg_ref, wet_ref, p8re_ref, p8im_ref):
    ns = SLAB_STATES
    b_re, b_im = wb_ref[0, :, 0:ns], wb_ref[0, :, ns:2 * ns]
    c_re, c_im = wc_ref[0, 0:ns, :], wc_ref[0, ns:2 * ns, :]
    ct_re, ct_im = wct_ref[0, :, 0:ns], wct_ref[0, :, ns:2 * ns]
    powers = [(jnp.ones((1, ns), F32), jnp.zeros((1, ns), F32))]
    powers += [(pre_ref[0, r:r + 1, :], pim_ref[0, r:r + 1, :]) for r in range(SUBLANES)]
    hi = lax.Precision.HIGHEST
    for t in range(SUBLANES):
        l_re, l_im = powers[t]
        lb_re = b_re * l_re - b_im * l_im
        lb_im = b_re * l_im + b_im * l_re
        k_t = (jnp.dot(lb_re, c_re, precision=hi, preferred_element_type=F32)
               - jnp.dot(lb_im, c_im, precision=hi, preferred_element_type=F32))
        kcat_ref[0, t * LANES:(t + 1) * LANES, :] = k_t.astype(kcat_ref.dtype)
        k = SUBLANES - 1 - t
        wg_ref[0, k * LANES:(k + 1) * LANES, 0:ns] = lb_re.astype(wg_ref.dtype)
        wg_ref[0, k * LANES:(k + 1) * LANES, ns:2 * ns] = lb_im.astype(wg_ref.dtype)
    for k in range(SUBLANES):
        l_re, l_im = powers[k + 1]
        wet_ref[0, k * LANES:(k + 1) * LANES, 0:ns] = (ct_re * l_re - ct_im * l_im).astype(wet_ref.dtype)
        wet_ref[0, k * LANES:(k + 1) * LANES, ns:2 * ns] = (-(ct_re * l_im + ct_im * l_re)).astype(wet_ref.dtype)
    m_re, m_im = powers[SUBLANES]
    q_re, q_im = m_re, m_im
    for r in range(SUBLANES):
        p8re_ref[0, r:r + 1, :] = q_re
        p8im_ref[0, r:r + 1, :] = q_im
        q_re, q_im = q_re * m_re - q_im * m_im, q_re * m_im + q_im * m_re


def _s5_ops(wb, wc, wct, pow_re, pow_im):
    n_slabs = wb.shape[0]
    ns = SLAB_STATES
    wide = SUBLANES * LANES
    slab = lambda *shape: pl.BlockSpec((1,) + shape, lambda j: (j, 0, 0))
    return pl.pallas_call(
        _s5_ops_body,
        out_shape=(
            jax.ShapeDtypeStruct((n_slabs, wide, LANES), BF16),
            jax.ShapeDtypeStruct((n_slabs, wide, 2 * ns), BF16),
            jax.ShapeDtypeStruct((n_slabs, wide, 2 * ns), BF16),
            jax.ShapeDtypeStruct((n_slabs, SUBLANES, ns), F32),
            jax.ShapeDtypeStruct((n_slabs, SUBLANES, ns), F32),
        ),
        grid=(n_slabs,),
        in_specs=[slab(LANES, 2 * ns), slab(2 * ns, LANES), slab(LANES, 2 * ns),
                  slab(SUBLANES, ns), slab(SUBLANES, ns)],
        out_specs=(slab(wide, LANES), slab(wide, 2 * ns), slab(wide, 2 * ns),
                   slab(SUBLANES, ns), slab(SUBLANES, ns)),
        compiler_params=_params("arbitrary"),
        name="s5_ops",
    )(wb, wc, wct, pow_re, pow_im)


def _s5_ssm_body(u_ref, kcat_ref, wg_ref, we_ref, pre_ref, pim_ref, y_ref,
                 ucat_ref, g_ref, h_ref, y1_ref, carry_ref):
    t_idx = pl.program_id(1)
    ns = SLAB_STATES
    rows = u_ref.shape[0]
    n_chunks = rows // SUBLANES
    slot = t_idx % 2

    @pl.when(t_idx == 0)
    def _():
        carry_ref[...] = jnp.zeros_like(carry_ref)
        g_ref[...] = jnp.zeros_like(g_ref)
        h_ref[...] = jnp.zeros_like(h_ref)
        y1_ref[...] = jnp.zeros_like(y1_ref)

    y_ref[...] = y1_ref[(t_idx + 1) % 3]
    h_in = h_ref[slot, pl.ds(SUBLANES - 1, n_chunks), :].astype(BF16)
    y2 = jnp.dot(h_in, we_ref[0], preferred_element_type=F32)
    for k in range(SUBLANES):
        y_ref[pl.ds(k, n_chunks, stride=SUBLANES), :] += y2[:, k * LANES:(k + 1) * LANES]

    u3 = u_ref[...].reshape(n_chunks, SUBLANES, LANES)
    row = lax.broadcasted_iota(jnp.int32, u3.shape, 1)
    for tau in range(SUBLANES):
        shifted = u3 if tau == 0 else jnp.where(row >= tau, pltpu.roll(u3, tau, 1), 0.0)
        ucat_ref[:, tau * LANES:(tau + 1) * LANES] = shifted.reshape(rows, LANES).astype(BF16)
    y1_ref[t_idx % 3] = jnp.dot(ucat_ref[...], kcat_ref[0], preferred_element_type=F32)

    p_re = pre_ref[0]
    p_im = pim_ref[0]
    srow = lax.broadcasted_iota(jnp.int32, (SUBLANES, ns), 0)
    steps = []
    for s in (1, 2, 4):
        keep = srow >= s
        steps.append((s,
                      jnp.where(keep, p_re[s - 1:s, :], 0.0),
                      jnp.where(keep, p_im[s - 1:s, :], 0.0)))
    g_old = g_ref.at[1 - slot]
    h_new = h_ref.at[1 - slot]
    h_new[0:SUBLANES, :] = carry_ref[...]
    c_re, c_im = carry_ref[:, 0:ns], carry_ref[:, ns:2 * ns]
    last = SUBLANES - 1
    for i in range(n_chunks // SUBLANES):
        r0 = i * SUBLANES
        x_re = g_old[r0:r0 + SUBLANES, 0:ns]
        x_im = g_old[r0:r0 + SUBLANES, ns:2 * ns]
        for s, a_re, a_im in steps:
            s_re = pltpu.roll(x_re, s, 0)
            s_im = pltpu.roll(x_im, s, 0)
            x_re, x_im = (x_re + (a_re * s_re - a_im * s_im),
                          x_im + (a_re * s_im + a_im * s_re))
        x_re, x_im = (x_re + (p_re * c_re - p_im * c_im),
                      x_im + (p_re * c_im + p_im * c_re))
        h_new[r0 + SUBLANES:r0 + 2 * SUBLANES, 0:ns] = x_re
        h_new[r0 + SUBLANES:r0 + 2 * SUBLANES, ns:2 * ns] = x_im
        c_re = jnp.broadcast_to(x_re[last:last + 1, :], (SUBLANES, ns))
        c_im = jnp.broadcast_to(x_im[last:last + 1, :], (SUBLANES, ns))
    carry_ref[:, 0:ns] = c_re
    carry_ref[:, ns:2 * ns] = c_im

    per_step = [u_ref[pl.ds(k, n_chunks, stride=SUBLANES), :].astype(BF16)
                for k in range(SUBLANES)]
    g_ref[slot] = jnp.dot(jnp.concatenate(per_step, axis=1), wg_ref[0],
                          preferred_element_type=F32)


def _s5_ssm(u, kcat, wg, we, pow8_re, pow8_im):
    n_slabs, seq, _ = u.shape
    ns = SLAB_STATES
    wide = SUBLANES * LANES
    rows = min(SSM_ROWS, seq)
    n_chunks = rows // SUBLANES
    n_tiles = seq // rows
    slab = lambda *shape: pl.BlockSpec((1,) + shape, lambda b, t: (b, 0, 0))
    return pl.pallas_call(
        _s5_ssm_body,
        out_shape=jax.ShapeDtypeStruct((n_slabs, seq, LANES), F32),
        grid=(n_slabs, n_tiles + 2),
        in_specs=[
            pl.BlockSpec((None, rows, LANES), lambda b, t: (b, jnp.minimum(t, n_tiles - 1), 0)),
            slab(wide, LANES), slab(wide, 2 * ns), slab(2 * ns, wide),
            slab(SUBLANES, ns), slab(SUBLANES, ns),
        ],
        out_specs=pl.BlockSpec((None, rows, LANES), lambda b, t: (b, jnp.maximum(t - 2, 0), 0)),
        scratch_shapes=[
            pltpu.VMEM((rows, wide), BF16),
            pltpu.VMEM((2, n_chunks, 2 * ns), F32),
            pltpu.VMEM((2, n_chunks + SUBLANES, 2 * ns), F32),
            pltpu.VMEM((3, rows, LANES), F32),
            pltpu.VMEM((SUBLANES, 2 * ns), F32),
        ],
        compiler_params=_params("arbitrary", "arbitrary"),
        name="s5_ssm",
    )(u, kcat, wg, we, pow8_re, pow8_im)


def _slab_block_diag(w):
    groups, a, b = w.shape
    n_slabs = groups // GROUPS_PER_SLAB
    eye = jnp.eye(GROUPS_PER_SLAB, dtype=w.dtype)
    w4 = w.reshape(n_slabs, GROUPS_PER_SLAB, a, b)
    out = jnp.einsum('jgab,gh->jgahb', w4, eye)
    return out.reshape(n_slabs, GROUPS_PER_SLAB * a, GROUPS_PER_SLAB * b)


def _s5_out_body(h_ref, y_ref, u_ref, d_ref, wglu_ref, wout_ref, o_ref, g_ref, gate_ref):
    step = pl.program_id(0)
    slot = step % 2
    n_slabs = y_ref.shape[0]

    @pl.when(step == 0)
    def _():
        g_ref[...] = jnp.zeros_like(g_ref)
        gate_ref[...] = jnp.zeros_like(gate_ref)

    g_old = g_ref[1 - slot]
    z = g_old * jax.nn.sigmoid(gate_ref[1 - slot])
    o_ref[...] = h_ref[...] + jnp.dot(z.astype(BF16), wout_ref[...], preferred_element_type=F32)

    y = jnp.concatenate([y_ref[s] for s in range(n_slabs)], axis=1)
    u = jnp.concatenate([u_ref[s] for s in range(n_slabs)], axis=1)
    g = jax.nn.gelu(y + d_ref[...] * u)
    g_ref[slot] = g
    gate_ref[slot] = jnp.dot(g.astype(BF16), wglu_ref[...], preferred_element_type=F32)


def _s5_out(h, y, u, d_skip, w_glu, w_out):
    seq, d = h.shape
    n_slabs = u.shape[0]
    width = n_slabs * LANES
    n_tiles = seq // PROJ_ROWS
    prev_rows = pl.BlockSpec((PROJ_ROWS, d), lambda i: (jnp.maximum(i - 1, 0), 0))
    cur_slabs = pl.BlockSpec((n_slabs, PROJ_ROWS, LANES),
                             lambda i: (0, jnp.minimum(i, n_tiles - 1), 0))
    once = lambda shape: pl.BlockSpec(shape, lambda i: (0, 0), pipeline_mode=pl.Buffered(1))
    return pl.pallas_call(
        _s5_out_body,
        out_shape=jax.ShapeDtypeStruct((seq, d), F32),
        grid=(n_tiles + 1,),
        in_specs=[prev_rows, cur_slabs, cur_slabs,
                  once((1, width)), once((width, width)), once((width, d))],
        out_specs=prev_rows,
        scratch_shapes=[pltpu.VMEM((2, PROJ_ROWS, width), F32),
                        pltpu.VMEM((2, PROJ_ROWS, width), F32)],
        compiler_params=_params("arbitrary"),
        name="s5_out",
    )(h, y, u, d_skip.astype(F32).reshape(1, width), w_glu.astype(BF16), w_out.astype(BF16))


def _s5_mixer(h, u, a_re, a_im, log_dt, b_re, b_im, c_re, c_im, d_skip, w_glu, w_out):
    groups, states = a_re.shape
    pow_re, pow_im, bb_re, bb_im = _s5_prep(a_re, a_im, log_dt, b_re, b_im)
    n_slabs = groups // GROUPS_PER_SLAB
    to_slabs = lambda p: p.reshape(SUBLANES, n_slabs, SLAB_STATES).transpose(1, 0, 2)
    per_group_b = lambda b: b.reshape(S5_GROUP, groups, states).transpose(1, 0, 2)
    wb = jnp.concatenate([_slab_block_diag(per_group_b(bb_re)),
                          _slab_block_diag(per_group_b(bb_im))], axis=-1)
    wct = jnp.concatenate([_slab_block_diag(c_re.astype(F32)),
                           _slab_block_diag(c_im.astype(F32))], axis=-1)
    wc = wct.transpose(0, 2, 1)
    kcat, wg, wet, pow8_re, pow8_im = _s5_ops(wb, wc, wct, to_slabs(pow_re), to_slabs(pow_im))
    y = _s5_ssm(u, kcat, wg, wet.transpose(0, 2, 1), pow8_re, pow8_im)
    return _s5_out(h, y, u, d_skip, w_glu, w_out)


def _head_dim_order():
    idx = list(range(ROT_HALF)) + list(range(ROT_DIM, HEAD_DIM // 2 + ROT_HALF))
    idx += list(range(ROT_HALF, ROT_DIM)) + list(range(HEAD_DIM // 2 + ROT_HALF, HEAD_DIM))
    return jnp.asarray(idx, jnp.int32)


def _rope(x, cos, sin_signed):
    return x * cos + pltpu.roll(x, HEAD_DIM // 2, 1) * sin_signed


def _qkv_body(x_ref, g_ref, w_ref, wvt_ref, qg_ref, kg_ref, invf_ref, q_ref, k_ref, vt_ref,
              cos_rel_ref, sin_rel_ref, y_ref):
    step = pl.program_id(0)
    rows = x_ref.shape[0]
    width = q_ref.shape[1]

    @pl.when(step == 0)
    def _():
        rel = lax.broadcasted_iota(jnp.int32, (rows, HEAD_DIM), 0).astype(F32) * invf_ref[...]
        cos_rel_ref[...] = jnp.cos(rel)
        sin_rel_ref[...] = jnp.sin(rel)
        y_ref[...] = jnp.zeros_like(y_ref)

    base = (jnp.maximum(step - 1, 0) * rows).astype(F32) * invf_ref[...]
    cos0, sin0 = jnp.cos(base), jnp.sin(base)
    cos = cos0 * cos_rel_ref[...] - sin0 * sin_rel_ref[...]
    sin = sin0 * cos_rel_ref[...] + cos0 * sin_rel_ref[...]
    lane = lax.broadcasted_iota(jnp.int32, (rows, HEAD_DIM), 1)
    half = HEAD_DIM // 2
    sin_signed = jnp.where(lane < ROT_HALF, -sin,
                           jnp.where((lane >= half) & (lane < half + ROT_HALF), sin, 0.0))

    def finish(part, gain, scale, out_ref):
        for hd in range(width // HEAD_DIM):
            cols = slice(hd * HEAD_DIM, (hd + 1) * HEAD_DIM)
            yh = y_ref[:, part * width + hd * HEAD_DIM:part * width + (hd + 1) * HEAD_DIM]
            yh = _rope(_rms_normed(yh, gain), cos, sin_signed)
            if scale is not None:
                yh = yh * scale
            out_ref[:, cols] = yh.astype(out_ref.dtype)

    finish(0, qg_ref[...], HEAD_DIM ** -0.5 * LOG2_E, q_ref)
    finish(1, kg_ref[...], None, k_ref)

    xn = _rms_normed(x_ref[...], g_ref[...]).astype(BF16)
    vt = lax.dot_general(wvt_ref[...], xn, (((1,), (1,)), ((), ())), preferred_element_type=F32)
    vt_ref[...] = vt.astype(vt_ref.dtype)
    y_ref[...] = jnp.dot(xn, w_ref[...], preferred_element_type=F32)


def _qkv(h, gain, w_qkv, q_gain, k_gain):
    seq, d = h.shape
    width = w_qkv.shape[1] // 3
    inv_freq = ROPE_THETA ** (-jnp.arange(0, ROT_DIM, 2, dtype=F32) / ROT_DIM)
    gap = jnp.zeros((HEAD_DIM // 2 - ROT_HALF,), F32)
    invf = jnp.concatenate([inv_freq, gap, inv_freq, gap]).reshape(1, HEAD_DIM)
    out = jax.ShapeDtypeStruct((seq, width), BF16)
    n_tiles = seq // PROJ_ROWS
    cur_rows = lambda n: pl.BlockSpec((PROJ_ROWS, n), lambda i: (jnp.minimum(i, n_tiles - 1), 0))
    prev_rows = lambda n: pl.BlockSpec((PROJ_ROWS, n), lambda i: (jnp.maximum(i - 1, 0), 0))
    const = lambda shape: pl.BlockSpec(shape, lambda i: (0, 0))
    order = _head_dim_order()
    cols = (jnp.arange(2 * width // HEAD_DIM, dtype=jnp.int32)[:, None] * HEAD_DIM
            + order[None, :]).reshape(-1)
    w_qk = w_qkv[:, :2 * width][:, cols].astype(BF16)
    w_vt = w_qkv[:, 2 * width:].T.astype(BF16)
    q_gain, k_gain = q_gain[order], k_gain[order]
    return pl.pallas_call(
        _qkv_body,
        out_shape=(out, out, jax.ShapeDtypeStruct((width, seq), BF16)),
        grid=(n_tiles + 1,),
        in_specs=[cur_rows(d), const((1, d)), const((d, 2 * width)), const((width, d)),
                  const((1, HEAD_DIM)), const((1, HEAD_DIM)), const((1, HEAD_DIM))],
        out_specs=(prev_rows(width), prev_rows(width),
                   pl.BlockSpec((width, PROJ_ROWS), lambda i: (0, jnp.minimum(i, n_tiles - 1)))),
        scratch_shapes=[pltpu.VMEM((PROJ_ROWS, HEAD_DIM), F32),
                        pltpu.VMEM((PROJ_ROWS, HEAD_DIM), F32),
                        pltpu.VMEM((PROJ_ROWS, 2 * width), F32)],
        compiler_params=_params("arbitrary"),
        name="moba_qkv",
    )(h, gain.reshape(1, d), w_qk, w_vt,
      q_gain.astype(F32).reshape(1, HEAD_DIM), k_gain.astype(F32).reshape(1, HEAD_DIM), invf)


def _attn_body(q_ref, k_ref, vt_ref, o_ref, kmean_ref, bias_ref, s_ref, mx_ref, p_ref,
               acc_ref, l_ref, alpha_ref):
    step = pl.program_id(1)
    n_tiles = pl.num_programs(1) - 1
    tile = jnp.minimum(step, n_tiles - 1)
    n_blocks = k_ref.shape[0] // MOBA_BLOCK
    n_q = q_ref.shape[0]
    top_block = tile * ATTN_TILE_BLOCKS + (ATTN_TILE_BLOCKS - 1)
    neg_inf = -jnp.inf

    @pl.when((pl.program_id(0) == 0) & (step == 0))
    def _():
        p_ref[...] = jnp.zeros(p_ref.shape, BF16)
        acc_ref[...] = jnp.zeros(acc_ref.shape, F32)
        l_ref[...] = jnp.ones(l_ref.shape, F32)
        alpha_ref[...] = jnp.ones(alpha_ref.shape, F32)

    @pl.when(step == 0)
    def _():
        for b in range(n_blocks):
            kb = k_ref[b * MOBA_BLOCK:(b + 1) * MOBA_BLOCK, :].astype(F32)
            kmean_ref[b:b + 1, :] = jnp.mean(kb, axis=0, keepdims=True)

    q = q_ref[...]
    nt = (((1,), (1,)), ((), ()))

    def groups_of(top):
        return (top + ATTN_GROUP) // ATTN_GROUP

    def block_of(group, g, top=top_block):
        return top - (group * ATTN_GROUP + g)

    def scores(block):
        kb0 = pl.multiple_of(jnp.maximum(block, 0) * MOBA_BLOCK, MOBA_BLOCK)
        return lax.dot_general(k_ref[pl.ds(kb0, MOBA_BLOCK), :], q, nt,
                               preferred_element_type=F32)

    def stage_scores(group, causal):
        for g in range(ATTN_GROUP):
            s = scores(block_of(group, g))
            if causal and g < ATTN_TILE_BLOCKS:
                k_pos = lax.broadcasted_iota(jnp.int32, s.shape, 0)
                q_pos = lax.broadcasted_iota(jnp.int32, s.shape, 1)
                q_rel = q_pos - (ATTN_TILE_BLOCKS - 1 - g) * MOBA_BLOCK
                s = jnp.where((q_rel >= MOBA_BLOCK) | (k_pos <= q_rel), s, neg_inf)
            s_ref[g] = s
            mx_ref[g:g + 1, :] = jnp.max(s, axis=0, keepdims=True)

    def stage_values(group, acc, alpha, top=top_block, weight=None):
        pv = None
        for g in range(ATTN_GROUP):
            b = jnp.clip(block_of(group, g, top), 0, n_blocks - 1)
            kb0 = pl.multiple_of(b * MOBA_BLOCK, MOBA_BLOCK)
            d = jnp.dot(vt_ref[:, pl.ds(kb0, MOBA_BLOCK)], p_ref[g], preferred_element_type=F32)
            pv = d if pv is None else pv + d
        if weight is not None:
            pv = pv * weight
        return alpha * acc + pv

    prev_top = top_block - ATTN_TILE_BLOCKS * jnp.where(step == n_tiles, 0, 1)
    prev_acc = stage_values(groups_of(prev_top) - 1, acc_ref[...], alpha_ref[...], prev_top)
    o_ref[...] = (prev_acc / l_ref[...]).T.astype(o_ref.dtype)

    stage_scores(0, causal=True)

    gate = lax.dot_general(kmean_ref[...].astype(BF16), q, nt, preferred_element_type=F32)
    blk = lax.broadcasted_iota(jnp.int32, gate.shape, 0)
    q_lane = lax.broadcasted_iota(jnp.int32, gate.shape, 1)
    own = tile * ATTN_TILE_BLOCKS + sum(
        (q_lane >= j * MOBA_BLOCK).astype(jnp.int32) for j in range(1, ATTN_TILE_BLOCKS))
    blk_f = blk.astype(F32)
    gate = jnp.where(blk < own, gate, neg_inf)
    selected = blk == own
    for j in range(min(MOBA_TOPK, n_blocks)):
        top = jnp.max(gate, axis=0, keepdims=True)
        first = jnp.min(jnp.where(gate == top, blk_f, float(n_blocks)), axis=0, keepdims=True)
        hit = blk_f == first
        selected = selected | (hit & (own > j))
        gate = jnp.where(hit, neg_inf, gate)
    bias_ref[0:n_blocks, :] = jnp.where(selected, 0.0, neg_inf)
    bias_ref[n_blocks:, :] = jnp.full((bias_ref.shape[0] - n_blocks, n_q), neg_inf, F32)

    def bias_row(group, g):
        b = block_of(group, g)
        return bias_ref[pl.ds(jnp.where(b < 0, n_blocks, b), 1), :]

    def trip(i, carry):
        m, l, acc, alpha_prev = carry
        acc = stage_values(i - 1, acc, alpha_prev, weight=jnp.where(i > 0, 1.0, 0.0))

        rows = [bias_row(i, g) for g in range(ATTN_GROUP)]
        m_new = m
        for g in range(ATTN_GROUP):
            m_new = jnp.maximum(m_new, mx_ref[g:g + 1, :] + rows[g])
        alpha = jnp.exp2(m - m_new)
        l = alpha * l
        for g in range(ATTN_GROUP):
            p = jnp.exp2(s_ref[g] - (m_new - rows[g]))
            l = l + jnp.sum(p, axis=0, keepdims=True)
            p_ref[g] = p.astype(BF16)

        stage_scores(i + 1, causal=False)
        return m_new, l, acc, alpha

    n_groups = jnp.where(step == n_tiles, 0, groups_of(top_block))
    init = (jnp.full((1, n_q), neg_inf, F32), jnp.zeros((1, n_q), F32),
            jnp.zeros((HEAD_DIM, n_q), F32), jnp.ones((1, n_q), F32))
    m, l, acc, alpha = lax.fori_loop(0, n_groups, trip, init)
    acc_ref[...] = acc
    l_ref[...] = jnp.where(l > 0.0, l, 1.0)
    alpha_ref[...] = alpha


def _attention(q, k, vt):
    seq, width = q.shape
    n_heads = width // HEAD_DIM
    n_blocks = seq // MOBA_BLOCK
    n_q = ATTN_TILE_BLOCKS * MOBA_BLOCK
    bias_rows = n_blocks + SUBLANES
    n_tiles = seq // n_q
    return pl.pallas_call(
        _attn_body,
        out_shape=jax.ShapeDtypeStruct((seq, width), BF16),
        grid=(n_heads, n_tiles + 1),
        in_specs=[
            pl.BlockSpec((n_q, HEAD_DIM), lambda hd, t: (jnp.minimum(t, n_tiles - 1), hd)),
            pl.BlockSpec((seq, HEAD_DIM), lambda hd, t: (0, hd)),
            pl.BlockSpec((HEAD_DIM, seq), lambda hd, t: (hd, 0)),
        ],
        out_specs=pl.BlockSpec((n_q, HEAD_DIM), lambda hd, t: (jnp.maximum(t - 1, 0), hd)),
        scratch_shapes=[pltpu.VMEM((n_blocks, HEAD_DIM), F32),
                        pltpu.VMEM((bias_rows, n_q), F32),
                        pltpu.VMEM((ATTN_GROUP, MOBA_BLOCK, n_q), F32),
                        pltpu.VMEM((ATTN_GROUP, n_q), F32),
                        pltpu.VMEM((ATTN_GROUP, MOBA_BLOCK, n_q), BF16),
                        pltpu.VMEM((HEAD_DIM, n_q), F32),
                        pltpu.VMEM((1, n_q), F32),
                        pltpu.VMEM((1, n_q), F32)],
        compiler_params=_params("arbitrary", "arbitrary"),
        name="moba_attn",
    )(q, k, vt)


def _moba_heads(h, gain, w_qkv, q_gain, k_gain):
    q, k, vt = _qkv(h, gain, w_qkv, q_gain, k_gain)
    return _attention(q, k, vt)


def kernel(x, ffn_norm, ffn_w_gate, ffn_w_up, ffn_w_down, mix_norm, s5_w_in, s5_a_re, s5_a_im, s5_log_dt, s5_b_re, s5_b_im, s5_c_re, s5_c_im, s5_d, s5_w_glu, s5_w_out, moba_w_qkv, moba_q_norm, moba_k_norm, moba_w_out):
    bsz, seq, d = x.shape
    depth = ffn_norm.shape[0]
    d_ff = ffn_w_gate.shape[-1]
    wg_all = ffn_w_gate.reshape(depth * 2, d, d_ff)
    wu_all = ffn_w_up.reshape(depth * 2, d, d_ff)
    wd_all = ffn_w_down.reshape(depth * 2, d_ff, d)
    outs = []
    for b in range(bsz):
        h = (x.reshape(seq, d) if bsz == 1 else x[b]).astype(F32)
        for layer in range(depth):
            i = layer // 2
            ffn = functools.partial(_ffn, w_gate=wg_all, w_up=wu_all, w_down=wd_all)
            if layer % 2 == 0:
                h, u = ffn(h, ffn_norm[layer, 0], idx=2 * layer,
                           post=(mix_norm[layer], s5_w_in[i]))
                h = _s5_mixer(h, u, s5_a_re[i], s5_a_im[i], s5_log_dt[i], s5_b_re[i],
                              s5_b_im[i], s5_c_re[i], s5_c_im[i], s5_d[i], s5_w_glu[i],
                              s5_w_out[i])
                h = ffn(h, ffn_norm[layer, 1], idx=2 * layer + 1)
            else:
                h = ffn(h, ffn_norm[layer, 0], idx=2 * layer)
                o = _moba_heads(h, mix_norm[layer], moba_w_qkv[i], moba_q_norm[i],
                                moba_k_norm[i])
                h = ffn(h, ffn_norm[layer, 1], idx=2 * layer + 1, pre=(o, moba_w_out[i]))
        outs.append(h)
    out = outs[0].reshape(1, seq, d) if bsz == 1 else jnp.stack(outs)
    return out.astype(x.dtype)
```

```python
import functools

import jax
import jax.numpy as jnp
from jax import lax
from jax.experimental import pallas as pl
from jax.experimental.pallas import tpu as pltpu

F32 = jnp.float32
BF16 = jnp.bfloat16

LANES = 128
SUBLANES = 8
VMEM_LIMIT_BYTES = 56 * 1024 * 1024

EPS = 1e-6
MACARON_WEIGHT = 0.5
S5_GROUP = 16
S5_STATE = 64
GROUPS_PER_SLAB = LANES // S5_GROUP
SLAB_STATES = GROUPS_PER_SLAB * S5_STATE
N_HEADS = 8
HEAD_DIM = 128
MOBA_BLOCK = 256
MOBA_TOPK = 3
ROPE_THETA = 500000.0
ROT_DIM = HEAD_DIM // 4
ROT_HALF = ROT_DIM // 2

FFN_ROWS = 512
FFN_CHUNK = 256
PROJ_ROWS = 512
SSM_ROWS = 2048
ATTN_TILE_BLOCKS = 2
ATTN_GROUP = 8
LOG2_E = 1.4426950408889634


def _params(*semantics):
    return pltpu.CompilerParams(dimension_semantics=semantics,
                                vmem_limit_bytes=VMEM_LIMIT_BYTES)


def _rms_normed(x, gain):
    ms = jnp.mean(x * x, axis=-1, keepdims=True)
    return x * lax.rsqrt(ms + EPS) * gain


def _ffn_body(*refs, has_pre, has_post):
    refs = list(refs)
    x_ref, g_ref, wg_ref, wu_ref, wd_ref = refs[:5]
    rest = refs[5:]
    if has_pre:
        a_ref, wpre_ref = rest[:2]
        rest = rest[2:]
    if has_post:
        gpost_ref, wpost_ref = rest[:2]
        rest = rest[2:]
    o_ref = rest[0]
    act_ref = rest[-1]

    x = x_ref[...]
    if has_pre:
        x = x + jnp.dot(a_ref[...], wpre_ref[...], preferred_element_type=F32)
    xn = _rms_normed(x, g_ref[...]).astype(BF16)
    d_ff = wg_ref.shape[1]
    for j in range(d_ff // FFN_CHUNK):
        cols = slice(j * FFN_CHUNK, (j + 1) * FFN_CHUNK)
        gate = jnp.dot(xn, wg_ref[:, cols].astype(BF16), preferred_element_type=F32)
        up = jnp.dot(xn, wu_ref[:, cols].astype(BF16), preferred_element_type=F32)
        act_ref[:, cols] = (gate * jax.nn.sigmoid(gate) * up).astype(BF16)
    y = jnp.dot(act_ref[...], wd_ref[...].astype(BF16), preferred_element_type=F32)
    out = x + MACARON_WEIGHT * y
    o_ref[...] = out
    if has_post:
        nxt_ref = rest[1]
        outn = _rms_normed(out, gpost_ref[...]).astype(BF16)
        nxt = jnp.dot(outn, wpost_ref[...], preferred_element_type=F32)
        for s in range(nxt_ref.shape[0]):
            nxt_ref[s] = nxt[:, s * LANES:(s + 1) * LANES]


def _ffn(h, gain, w_gate, w_up, w_down, idx, pre=None, post=None):
    seq, d = h.shape
    d_ff = w_gate.shape[2]
    rows = lambda n: pl.BlockSpec((FFN_ROWS, n), lambda i: (i, 0))
    once = lambda shape: pl.BlockSpec(shape, lambda i: (0,) * len(shape),
                                      pipeline_mode=pl.Buffered(1))
    stacked = lambda a, b: pl.BlockSpec((None, a, b), lambda i: (idx, 0, 0),
                                        pipeline_mode=pl.Buffered(1))
    args = [h, gain.reshape(1, d), w_gate, w_up, w_down]
    in_specs = [rows(d), once((1, d)), stacked(d, d_ff), stacked(d, d_ff), stacked(d_ff, d)]
    if pre is not None:
        a, w_pre = pre
        args += [a, w_pre.astype(BF16)]
        in_specs += [rows(a.shape[1]), once(w_pre.shape)]
    out_shape = jax.ShapeDtypeStruct((seq, d), F32)
    out_specs = rows(d)
    if post is not None:
        g_post, w_post = post
        args += [g_post.reshape(1, d), w_post.astype(BF16)]
        in_specs += [once((1, d)), once(w_post.shape)]
        n_slabs = w_post.shape[1] // LANES
        out_shape = (out_shape, jax.ShapeDtypeStruct((n_slabs, seq, LANES), F32))
        out_specs = (out_specs, pl.BlockSpec((n_slabs, FFN_ROWS, LANES), lambda i: (0, i, 0)))
    return pl.pallas_call(
        functools.partial(_ffn_body, has_pre=pre is not None, has_post=post is not None),
        out_shape=out_shape,
        grid=(seq // FFN_ROWS,),
        in_specs=in_specs,
        out_specs=out_specs,
        scratch_shapes=[pltpu.VMEM((FFN_ROWS, d_ff), BF16)],
        compiler_params=_params("arbitrary"),
        name="ffn",
    )(*args)


def _s5_prep_body(are_ref, aim_ref, ldt_ref, bre_ref, bim_ref,
                  pre_ref, pim_ref, bbre_ref, bbim_ref):
    a_re = are_ref[...]
    a_im = aim_ref[...]
    dt = jnp.exp(ldt_ref[...])
    mag = jnp.exp(a_re * dt)
    lam_re = mag * jnp.cos(a_im * dt)
    lam_im = mag * jnp.sin(a_im * dt)
    den = a_re * a_re + a_im * a_im
    num_re = lam_re - 1.0
    coef_re = (num_re * a_re + lam_im * a_im) / den
    coef_im = (lam_im * a_re - num_re * a_im) / den
    b_re = bre_ref[...]
    b_im = bim_ref[...]
    bbre_ref[...] = coef_re * b_re - coef_im * b_im
    bbim_ref[...] = coef_re * b_im + coef_im * b_re
    p_re, p_im = lam_re, lam_im
    for r in range(SUBLANES):
        pre_ref[r:r + 1, :] = p_re
        pim_ref[r:r + 1, :] = p_im
        p_re, p_im = p_re * lam_re - p_im * lam_im, p_re * lam_im + p_im * lam_re


def _s5_prep(a_re, a_im, log_dt, b_re, b_im):
    groups, states = a_re.shape
    n_flat = groups * states
    row = lambda a: a.astype(F32).reshape(1, n_flat)
    ldt = jnp.broadcast_to(log_dt.astype(F32)[:, None], (groups, states)).reshape(1, n_flat)
    chan_rows = lambda b: b.astype(F32).transpose(2, 0, 1).reshape(S5_GROUP, n_flat)
    return pl.pallas_call(
        _s5_prep_body,
        out_shape=(
            jax.ShapeDtypeStruct((SUBLANES, n_flat), F32),
            jax.ShapeDtypeStruct((SUBLANES, n_flat), F32),
            jax.ShapeDtypeStruct((S5_GROUP, n_flat), F32),
            jax.ShapeDtypeStruct((S5_GROUP, n_flat), F32),
        ),
        name="s5_prep",
    )(row(a_re), row(a_im), ldt, chan_rows(b_re), chan_rows(b_im))


def _s5_ops_body(wb_ref, wc_ref, wct_ref, pre_ref, pim_ref,
                 kcat_ref, wg_ref, wet_ref, p8re_ref, p8im_ref):
    ns = SLAB_STATES
    b_re, b_im = wb_ref[0, :, 0:ns], wb_ref[0, :, ns:2 * ns]
    c_re, c_im = wc_ref[0, 0:ns, :], wc_ref[0, ns:2 * ns, :]
    ct_re, ct_im = wct_ref[0, :, 0:ns], wct_ref[0, :, ns:2 * ns]
    powers = [(jnp.ones((1, ns), F32), jnp.zeros((1, ns), F32))]
    powers += [(pre_ref[0, r:r + 1, :], pim_ref[0, r:r + 1, :]) for r in range(SUBLANES)]
    hi = lax.Precision.HIGHEST
    for t in range(SUBLANES):
        l_re, l_im = powers[t]
        lb_re = b_re * l_re - b_im * l_im
        lb_im = b_re * l_im + b_im * l_re
        k_t = (jnp.dot(lb_re, c_re, precision=hi, preferred_element_type=F32)
               - jnp.dot(lb_im, c_im, precision=hi, preferred_element_type=F32))
        kcat_ref[0, t * LANES:(t + 1) * LANES, :] = k_t.astype(kcat_ref.dtype)
        k = SUBLANES - 1 - t
        wg_ref[0, k * LANES:(k + 1) * LANES, 0:ns] = lb_re.astype(wg_ref.dtype)
        wg_ref[0, k * LANES:(k + 1) * LANES, ns:2 * ns] = lb_im.astype(wg_ref.dtype)
    for k in range(SUBLANES):
        l_re, l_im = powers[k + 1]
        wet_ref[0, k * LANES:(k + 1) * LANES, 0:ns] = (ct_re * l_re - ct_im * l_im).astype(wet_ref.dtype)
        wet_ref[0, k * LANES:(k + 1) * LANES, ns:2 * ns] = (-(ct_re * l_im + ct_im * l_re)).astype(wet_ref.dtype)
    m_re, m_im = powers[SUBLANES]
    q_re, q_im = m_re, m_im
    for r in range(SUBLANES):
        p8re_ref[0, r:r + 1, :] = q_re
        p8im_ref[0, r:r + 1, :] = q_im
        q_re, q_im = q_re * m_re - q_im * m_im, q_re * m_im + q_im * m_re


def _s5_ops(wb, wc, wct, pow_re, pow_im):
    n_slabs = wb.shape[0]
    ns = SLAB_STATES
    wide = SUBLANES * LANES
    slab = lambda *shape: pl.BlockSpec((1,) + shape, lambda j: (j, 0, 0))
    return pl.pallas_call(
        _s5_ops_body,
        out_shape=(
            jax.ShapeDtypeStruct((n_slabs, wide, LANES), BF16),
            jax.ShapeDtypeStruct((n_slabs, wide, 2 * ns), BF16),
            jax.ShapeDtypeStruct((n_slabs, wide, 2 * ns), BF16),
            jax.ShapeDtypeStruct((n_slabs, SUBLANES, ns), F32),
            jax.ShapeDtypeStruct((n_slabs, SUBLANES, ns), F32),
        ),
        grid=(n_slabs,),
        in_specs=[slab(LANES, 2 * ns), slab(2 * ns, LANES), slab(LANES, 2 * ns),
                  slab(SUBLANES, ns), slab(SUBLANES, ns)],
        out_specs=(slab(wide, LANES), slab(wide, 2 * ns), slab(wide, 2 * ns),
                   slab(SUBLANES, ns), slab(SUBLANES, ns)),
        compiler_params=_params("arbitrary"),
        name="s5_ops",
    )(wb, wc, wct, pow_re, pow_im)


def _s5_ssm_body(u_ref, uold_ref, kcat_ref, wg_ref, wet_ref, pre_ref, pim_ref, y_ref,
                 ucat_ref, g_ref, h_ref, carry_ref):
    t_idx = pl.program_id(1)
    ns = SLAB_STATES
    rows = u_ref.shape[0]
    n_chunks = rows // SUBLANES
    slot = t_idx % 2

    @pl.when(t_idx == 0)
    def _():
        carry_ref[...] = jnp.zeros_like(carry_ref)
        g_ref[...] = jnp.zeros_like(g_ref)
        h_ref[...] = jnp.zeros_like(h_ref)

    u3 = uold_ref[...].reshape(n_chunks, SUBLANES, LANES)
    row = lax.broadcasted_iota(jnp.int32, u3.shape, 1)
    for tau in range(SUBLANES):
        shifted = u3 if tau == 0 else jnp.where(row >= tau, pltpu.roll(u3, tau, 1), 0.0)
        ucat_ref[:, tau * LANES:(tau + 1) * LANES] = shifted.reshape(rows, LANES).astype(BF16)
    y_ref[...] = jnp.dot(ucat_ref[...], kcat_ref[0], preferred_element_type=F32)
    h_in = h_ref[slot, pl.ds(SUBLANES - 1, n_chunks), :].astype(BF16)
    y2 = lax.dot_general(h_in, wet_ref[0], (((1,), (1,)), ((), ())),
                         preferred_element_type=F32)
    for k in range(SUBLANES):
        y_ref[pl.ds(k, n_chunks, stride=SUBLANES), :] += y2[:, k * LANES:(k + 1) * LANES]

    p_re = pre_ref[0]
    p_im = pim_ref[0]
    srow = lax.broadcasted_iota(jnp.int32, (SUBLANES, ns), 0)
    steps = []
    for s in (1, 2, 4):
        keep = srow >= s
        steps.append((s,
                      jnp.where(keep, p_re[s - 1:s, :], 0.0),
                      jnp.where(keep, p_im[s - 1:s, :], 0.0)))
    g_old = g_ref.at[1 - slot]
    h_new = h_ref.at[1 - slot]
    h_new[0:SUBLANES, :] = carry_ref[...]
    c_re, c_im = carry_ref[:, 0:ns], carry_ref[:, ns:2 * ns]
    last = SUBLANES - 1
    for i in range(n_chunks // SUBLANES):
        r0 = i * SUBLANES
        x_re = g_old[r0:r0 + SUBLANES, 0:ns]
        x_im = g_old[r0:r0 + SUBLANES, ns:2 * ns]
        for s, a_re, a_im in steps:
            s_re = pltpu.roll(x_re, s, 0)
            s_im = pltpu.roll(x_im, s, 0)
            x_re, x_im = (x_re + (a_re * s_re - a_im * s_im),
                          x_im + (a_re * s_im + a_im * s_re))
        x_re, x_im = (x_re + (p_re * c_re - p_im * c_im),
                      x_im + (p_re * c_im + p_im * c_re))
        h_new[r0 + SUBLANES:r0 + 2 * SUBLANES, 0:ns] = x_re
        h_new[r0 + SUBLANES:r0 + 2 * SUBLANES, ns:2 * ns] = x_im
        c_re = jnp.broadcast_to(x_re[last:last + 1, :], (SUBLANES, ns))
        c_im = jnp.broadcast_to(x_im[last:last + 1, :], (SUBLANES, ns))
    carry_ref[:, 0:ns] = c_re
    carry_ref[:, ns:2 * ns] = c_im

    per_step = [u_ref[pl.ds(k, n_chunks, stride=SUBLANES), :].astype(BF16)
                for k in range(SUBLANES)]
    g_ref[slot] = jnp.dot(jnp.concatenate(per_step, axis=1), wg_ref[0],
                          preferred_element_type=F32)


def _s5_ssm(u, kcat, wg, wet, pow8_re, pow8_im):
    n_slabs, seq, _ = u.shape
    ns = SLAB_STATES
    wide = SUBLANES * LANES
    rows = min(SSM_ROWS, seq)
    n_chunks = rows // SUBLANES
    n_tiles = seq // rows
    slab = lambda *shape: pl.BlockSpec((1,) + shape, lambda b, t: (b, 0, 0))
    return pl.pallas_call(
        _s5_ssm_body,
        out_shape=jax.ShapeDtypeStruct((n_slabs, seq, LANES), F32),
        grid=(n_slabs, n_tiles + 2),
        in_specs=[
            pl.BlockSpec((None, rows, LANES), lambda b, t: (b, jnp.minimum(t, n_tiles - 1), 0)),
            pl.BlockSpec((None, rows, LANES), lambda b, t: (b, jnp.maximum(t - 2, 0), 0)),
            slab(wide, LANES), slab(wide, 2 * ns), slab(wide, 2 * ns),
            slab(SUBLANES, ns), slab(SUBLANES, ns),
        ],
        out_specs=pl.BlockSpec((None, rows, LANES), lambda b, t: (b, jnp.maximum(t - 2, 0), 0)),
        scratch_shapes=[
            pltpu.VMEM((rows, wide), BF16),
            pltpu.VMEM((2, n_chunks, 2 * ns), F32),
            pltpu.VMEM((2, n_chunks + SUBLANES, 2 * ns), F32),
            pltpu.VMEM((SUBLANES, 2 * ns), F32),
        ],
        compiler_params=_params("arbitrary", "arbitrary"),
        name="s5_ssm",
    )(u, u, kcat, wg, wet, pow8_re, pow8_im)


def _slab_block_diag(w):
    groups, a, b = w.shape
    n_slabs = groups // GROUPS_PER_SLAB
    eye = jnp.eye(GROUPS_PER_SLAB, dtype=w.dtype)
    w4 = w.reshape(n_slabs, GROUPS_PER_SLAB, a, b)
    out = jnp.einsum('jgab,gh->jgahb', w4, eye)
    return out.reshape(n_slabs, GROUPS_PER_SLAB * a, GROUPS_PER_SLAB * b)


def _s5_out_body(h_ref, y_ref, u_ref, d_ref, wglu_ref, wout_ref, o_ref, g_ref, gate_ref):
    step = pl.program_id(0)
    slot = step % 2
    n_slabs = y_ref.shape[0]

    @pl.when(step == 0)
    def _():
        g_ref[...] = jnp.zeros_like(g_ref)
        gate_ref[...] = jnp.zeros_like(gate_ref)

    g_old = g_ref[1 - slot]
    z = g_old * jax.nn.sigmoid(gate_ref[1 - slot])
    o_ref[...] = h_ref[...] + jnp.dot(z.astype(BF16), wout_ref[...], preferred_element_type=F32)

    y = jnp.concatenate([y_ref[s] for s in range(n_slabs)], axis=1)
    u = jnp.concatenate([u_ref[s] for s in range(n_slabs)], axis=1)
    g = jax.nn.gelu(y + d_ref[...] * u)
    g_ref[slot] = g
    gate_ref[slot] = jnp.dot(g.astype(BF16), wglu_ref[...], preferred_element_type=F32)


def _s5_out(h, y, u, d_skip, w_glu, w_out):
    seq, d = h.shape
    n_slabs = u.shape[0]
    width = n_slabs * LANES
    n_tiles = seq // PROJ_ROWS
    prev_rows = pl.BlockSpec((PROJ_ROWS, d), lambda i: (jnp.maximum(i - 1, 0), 0))
    cur_slabs = pl.BlockSpec((n_slabs, PROJ_ROWS, LANES),
                             lambda i: (0, jnp.minimum(i, n_tiles - 1), 0))
    once = lambda shape: pl.BlockSpec(shape, lambda i: (0, 0), pipeline_mode=pl.Buffered(1))
    return pl.pallas_call(
        _s5_out_body,
        out_shape=jax.ShapeDtypeStruct((seq, d), F32),
        grid=(n_tiles + 1,),
        in_specs=[prev_rows, cur_slabs, cur_slabs,
                  once((1, width)), once((width, width)), once((width, d))],
        out_specs=prev_rows,
        scratch_shapes=[pltpu.VMEM((2, PROJ_ROWS, width), F32),
                        pltpu.VMEM((2, PROJ_ROWS, width), F32)],
        compiler_params=_params("arbitrary"),
        name="s5_out",
    )(h, y, u, d_skip.astype(F32).reshape(1, width), w_glu.astype(BF16), w_out.astype(BF16))


def _s5_mixer(h, u, a_re, a_im, log_dt, b_re, b_im, c_re, c_im, d_skip, w_glu, w_out):
    groups, states = a_re.shape
    pow_re, pow_im, bb_re, bb_im = _s5_prep(a_re, a_im, log_dt, b_re, b_im)
    n_slabs = groups // GROUPS_PER_SLAB
    to_slabs = lambda p: p.reshape(SUBLANES, n_slabs, SLAB_STATES).transpose(1, 0, 2)
    per_group_b = lambda b: b.reshape(S5_GROUP, groups, states).transpose(1, 0, 2)
    wb = jnp.concatenate([_slab_block_diag(per_group_b(bb_re)),
                          _slab_block_diag(per_group_b(bb_im))], axis=-1)
    wct = jnp.concatenate([_slab_block_diag(c_re.astype(F32)),
                           _slab_block_diag(c_im.astype(F32))], axis=-1)
    wc = wct.transpose(0, 2, 1)
    kcat, wg, wet, pow8_re, pow8_im = _s5_ops(wb, wc, wct, to_slabs(pow_re), to_slabs(pow_im))
    y = _s5_ssm(u, kcat, wg, wet, pow8_re, pow8_im)
    return _s5_out(h, y, u, d_skip, w_glu, w_out)


def _head_dim_order():
    idx = list(range(ROT_HALF)) + list(range(ROT_DIM, HEAD_DIM // 2 + ROT_HALF))
    idx += list(range(ROT_HALF, ROT_DIM)) + list(range(HEAD_DIM // 2 + ROT_HALF, HEAD_DIM))
    return jnp.asarray(idx, jnp.int32)


def _rope(x, cos, sin_signed):
    return x * cos + pltpu.roll(x, HEAD_DIM // 2, 1) * sin_signed


def _qkv_body(x_ref, g_ref, w_ref, wvt_ref, qg_ref, kg_ref, invf_ref, q_ref, k_ref, vt_ref,
              cos_rel_ref, sin_rel_ref, y_ref):
    step = pl.program_id(0)
    rows = x_ref.shape[0]
    width = q_ref.shape[1]

    @pl.when(step == 0)
    def _():
        rel = lax.broadcasted_iota(jnp.int32, (rows, HEAD_DIM), 0).astype(F32) * invf_ref[...]
        cos_rel_ref[...] = jnp.cos(rel)
        sin_rel_ref[...] = jnp.sin(rel)
        y_ref[...] = jnp.zeros_like(y_ref)

    base = (jnp.maximum(step - 1, 0) * rows).astype(F32) * invf_ref[...]
    cos0, sin0 = jnp.cos(base), jnp.sin(base)
    cos = cos0 * cos_rel_ref[...] - sin0 * sin_rel_ref[...]
    sin = sin0 * cos_rel_ref[...] + cos0 * sin_rel_ref[...]
    lane = lax.broadcasted_iota(jnp.int32, (rows, HEAD_DIM), 1)
    half = HEAD_DIM // 2
    sin_signed = jnp.where(lane < ROT_HALF, -sin,
                           jnp.where((lane >= half) & (lane < half + ROT_HALF), sin, 0.0))

    def finish(part, gain, scale, out_ref):
        for hd in range(width // HEAD_DIM):
            cols = slice(hd * HEAD_DIM, (hd + 1) * HEAD_DIM)
            yh = y_ref[:, part * width + hd * HEAD_DIM:part * width + (hd + 1) * HEAD_DIM]
            yh = _rope(_rms_normed(yh, gain), cos, sin_signed)
            if scale is not None:
                yh = yh * scale
            out_ref[:, cols] = yh.astype(out_ref.dtype)

    finish(0, qg_ref[...], HEAD_DIM ** -0.5 * LOG2_E, q_ref)
    finish(1, kg_ref[...], None, k_ref)

    xn = _rms_normed(x_ref[...], g_ref[...]).astype(BF16)
    vt = lax.dot_general(wvt_ref[...], xn, (((1,), (1,)), ((), ())), preferred_element_type=F32)
    vt_ref[...] = vt.astype(vt_ref.dtype)
    y_ref[...] = jnp.dot(xn, w_ref[...], preferred_element_type=F32)


def _qkv(h, gain, w_qkv, q_gain, k_gain):
    seq, d = h.shape
    width = w_qkv.shape[1] // 3
    inv_freq = ROPE_THETA ** (-jnp.arange(0, ROT_DIM, 2, dtype=F32) / ROT_DIM)
    gap = jnp.zeros((HEAD_DIM // 2 - ROT_HALF,), F32)
    invf = jnp.concatenate([inv_freq, gap, inv_freq, gap]).reshape(1, HEAD_DIM)
    out = jax.ShapeDtypeStruct((seq, width), BF16)
    n_tiles = seq // PROJ_ROWS
    cur_rows = lambda n: pl.BlockSpec((PROJ_ROWS, n), lambda i: (jnp.minimum(i, n_tiles - 1), 0))
    prev_rows = lambda n: pl.BlockSpec((PROJ_ROWS, n), lambda i: (jnp.maximum(i - 1, 0), 0))
    const = lambda shape: pl.BlockSpec(shape, lambda i: (0, 0))
    order = _head_dim_order()
    cols = (jnp.arange(2 * width // HEAD_DIM, dtype=jnp.int32)[:, None] * HEAD_DIM
            + order[None, :]).reshape(-1)
    w_qk = w_qkv[:, :2 * width][:, cols].astype(BF16)
    w_vt = w_qkv[:, 2 * width:].T.astype(BF16)
    q_gain, k_gain = q_gain[order], k_gain[order]
    return pl.pallas_call(
        _qkv_body,
        out_shape=(out, out, jax.ShapeDtypeStruct((width, seq), BF16)),
        grid=(n_tiles + 1,),
        in_specs=[cur_rows(d), const((1, d)), const((d, 2 * width)), const((width, d)),
                  const((1, HEAD_DIM)), const((1, HEAD_DIM)), const((1, HEAD_DIM))],
        out_specs=(prev_rows(width), prev_rows(width),
                   pl.BlockSpec((width, PROJ_ROWS), lambda i: (0, jnp.minimum(i, n_tiles - 1)))),
        scratch_shapes=[pltpu.VMEM((PROJ_ROWS, HEAD_DIM), F32),
                        pltpu.VMEM((PROJ_ROWS, HEAD_DIM), F32),
                        pltpu.VMEM((PROJ_ROWS, 2 * width), F32)],
        compiler_params=_params("arbitrary"),
        name="moba_qkv",
    )(h, gain.reshape(1, d), w_qk, w_vt,
      q_gain.astype(F32).reshape(1, HEAD_DIM), k_gain.astype(F32).reshape(1, HEAD_DIM), invf)


def _attn_body(q_ref, k_ref, vt_ref, o_ref, kmean_ref, bias_ref, s_ref, mx_ref, p_ref,
               acc_ref, l_ref, alpha_ref):
    step = pl.program_id(1)
    n_tiles = pl.num_programs(1) - 1
    tile = jnp.minimum(step, n_tiles - 1)
    n_blocks = k_ref.shape[0] // MOBA_BLOCK
    n_q = q_ref.shape[0]
    top_block = tile * ATTN_TILE_BLOCKS + (ATTN_TILE_BLOCKS - 1)
    neg_inf = -jnp.inf

    @pl.when((pl.program_id(0) == 0) & (step == 0))
    def _():
        p_ref[...] = jnp.zeros(p_ref.shape, BF16)
        acc_ref[...] = jnp.zeros(acc_ref.shape, F32)
        l_ref[...] = jnp.ones(l_ref.shape, F32)
        alpha_ref[...] = jnp.ones(alpha_ref.shape, F32)

    @pl.when(step == 0)
    def _():
        for b in range(n_blocks):
            kb = k_ref[b * MOBA_BLOCK:(b + 1) * MOBA_BLOCK, :].astype(F32)
            kmean_ref[b:b + 1, :] = jnp.mean(kb, axis=0, keepdims=True)

    q = q_ref[...]
    nt = (((1,), (1,)), ((), ()))

    def groups_of(top):
        return (top + ATTN_GROUP) // ATTN_GROUP

    def block_of(group, g, top=top_block):
        return top - (group * ATTN_GROUP + g)

    def scores(block):
        kb0 = pl.multiple_of(jnp.maximum(block, 0) * MOBA_BLOCK, MOBA_BLOCK)
        return lax.dot_general(k_ref[pl.ds(kb0, MOBA_BLOCK), :], q, nt,
                               preferred_element_type=F32)

    def stage_scores(group, causal):
        for g in range(ATTN_GROUP):
            s = scores(block_of(group, g))
            if causal and g < ATTN_TILE_BLOCKS:
                k_pos = lax.broadcasted_iota(jnp.int32, s.shape, 0)
                q_pos = lax.broadcasted_iota(jnp.int32, s.shape, 1)
                q_rel = q_pos - (ATTN_TILE_BLOCKS - 1 - g) * MOBA_BLOCK
                s = jnp.where((q_rel >= MOBA_BLOCK) | (k_pos <= q_rel), s, neg_inf)
            s_ref[g] = s
            mx_ref[g:g + 1, :] = jnp.max(s, axis=0, keepdims=True)

    def stage_values(group, acc, alpha, top=top_block, weight=None):
        pv = None
        for g in range(ATTN_GROUP):
            b = jnp.clip(block_of(group, g, top), 0, n_blocks - 1)
            kb0 = pl.multiple_of(b * MOBA_BLOCK, MOBA_BLOCK)
            d = jnp.dot(vt_ref[:, pl.ds(kb0, MOBA_BLOCK)], p_ref[g], preferred_element_type=F32)
            pv = d if pv is None else pv + d
        if weight is not None:
            pv = pv * weight
        return alpha * acc + pv

    prev_top = top_block - ATTN_TILE_BLOCKS * jnp.where(step == n_tiles, 0, 1)
    prev_acc = stage_values(groups_of(prev_top) - 1, acc_ref[...], alpha_ref[...], prev_top)
    o_ref[...] = (prev_acc / l_ref[...]).T.astype(o_ref.dtype)

    stage_scores(0, causal=True)

    gate = lax.dot_general(kmean_ref[...].astype(BF16), q, nt, preferred_element_type=F32)
    blk = lax.broadcasted_iota(jnp.int32, gate.shape, 0)
    q_lane = lax.broadcasted_iota(jnp.int32, gate.shape, 1)
    own = tile * ATTN_TILE_BLOCKS + sum(
        (q_lane >= j * MOBA_BLOCK).astype(jnp.int32) for j in range(1, ATTN_TILE_BLOCKS))
    blk_f = blk.astype(F32)
    gate = jnp.where(blk < own, gate, neg_inf)
    selected = blk == own
    for j in range(min(MOBA_TOPK, n_blocks)):
        top = jnp.max(gate, axis=0, keepdims=True)
        first = jnp.min(jnp.where(gate == top, blk_f, float(n_blocks)), axis=0, keepdims=True)
        hit = blk_f == first
        selected = selected | (hit & (own > j))
        gate = jnp.where(hit, neg_inf, gate)
    bias_ref[0:n_blocks, :] = jnp.where(selected, 0.0, neg_inf)
    bias_ref[n_blocks:, :] = jnp.full((bias_ref.shape[0] - n_blocks, n_q), neg_inf, F32)

    def bias_row(group, g):
        b = block_of(group, g)
        return bias_ref[pl.ds(jnp.where(b < 0, n_blocks, b), 1), :]

    def trip(i, carry):
        m, l, acc, alpha_prev = carry
        acc = stage_values(i - 1, acc, alpha_prev, weight=jnp.where(i > 0, 1.0, 0.0))

        rows = [bias_row(i, g) for g in range(ATTN_GROUP)]
        m_new = m
        for g in range(ATTN_GROUP):
            m_new = jnp.maximum(m_new, mx_ref[g:g + 1, :] + rows[g])
        alpha = jnp.exp2(m - m_new)
        l = alpha * l
        for g in range(ATTN_GROUP):
            p = jnp.exp2(s_ref[g] - (m_new - rows[g]))
            l = l + jnp.sum(p, axis=0, keepdims=True)
            p_ref[g] = p.astype(BF16)

        stage_scores(i + 1, causal=False)
        return m_new, l, acc, alpha

    n_groups = jnp.where(step == n_tiles, 0, groups_of(top_block))
    init = (jnp.full((1, n_q), neg_inf, F32), jnp.zeros((1, n_q), F32),
            jnp.zeros((HEAD_DIM, n_q), F32), jnp.ones((1, n_q), F32))
    m, l, acc, alpha = lax.fori_loop(0, n_groups, trip, init)
    acc_ref[...] = acc
    l_ref[...] = jnp.where(l > 0.0, l, 1.0)
    alpha_ref[...] = alpha


def _attention(q, k, vt):
    seq, width = q.shape
    n_heads = width // HEAD_DIM
    n_blocks = seq // MOBA_BLOCK
    n_q = ATTN_TILE_BLOCKS * MOBA_BLOCK
    bias_rows = n_blocks + SUBLANES
    n_tiles = seq // n_q
    return pl.pallas_call(
        _attn_body,
        out_shape=jax.ShapeDtypeStruct((seq, width), BF16),
        grid=(n_heads, n_tiles + 1),
        in_specs=[
            pl.BlockSpec((n_q, HEAD_DIM), lambda hd, t: (jnp.minimum(t, n_tiles - 1), hd)),
            pl.BlockSpec((seq, HEAD_DIM), lambda hd, t: (0, hd)),
            pl.BlockSpec((HEAD_DIM, seq), lambda hd, t: (hd, 0)),
        ],
        out_specs=pl.BlockSpec((n_q, HEAD_DIM), lambda hd, t: (jnp.maximum(t - 1, 0), hd)),
        scratch_shapes=[pltpu.VMEM((n_blocks, HEAD_DIM), F32),
                        pltpu.VMEM((bias_rows, n_q), F32),
                        pltpu.VMEM((ATTN_GROUP, MOBA_BLOCK, n_q), F32),
                        pltpu.VMEM((ATTN_GROUP, n_q), F32),
                        pltpu.VMEM((ATTN_GROUP, MOBA_BLOCK, n_q), BF16),
                        pltpu.VMEM((HEAD_DIM, n_q), F32),
                        pltpu.VMEM((1, n_q), F32),
                        pltpu.VMEM((1, n_q), F32)],
        compiler_params=_params("arbitrary", "arbitrary"),
        name="moba_attn",
    )(q, k, vt)


def _moba_heads(h, gain, w_qkv, q_gain, k_gain):
    q, k, vt = _qkv(h, gain, w_qkv, q_gain, k_gain)
    return _attention(q, k, vt)


def kernel(x, ffn_norm, ffn_w_gate, ffn_w_up, ffn_w_down, mix_norm, s5_w_in, s5_a_re, s5_a_im, s5_log_dt, s5_b_re, s5_b_im, s5_c_re, s5_c_im, s5_d, s5_w_glu, s5_w_out, moba_w_qkv, moba_q_norm, moba_k_norm, moba_w_out):
    bsz, seq, d = x.shape
    depth = ffn_norm.shape[0]
    d_ff = ffn_w_gate.shape[-1]
    assert seq % (ATTN_TILE_BLOCKS * MOBA_BLOCK) == 0 and seq % FFN_ROWS == 0
    assert seq % min(SSM_ROWS, seq) == 0 and seq % PROJ_ROWS == 0
    assert d % LANES == 0 and d_ff % FFN_CHUNK == 0
    assert s5_a_re.shape[1:] == (d // S5_GROUP, S5_STATE)
    assert moba_w_qkv.shape[-1] == 3 * d and d % HEAD_DIM == 0
    wg_all = ffn_w_gate.reshape(depth * 2, d, d_ff)
    wu_all = ffn_w_up.reshape(depth * 2, d, d_ff)
    wd_all = ffn_w_down.reshape(depth * 2, d_ff, d)
    outs = []
    for b in range(bsz):
        h = (x.reshape(seq, d) if bsz == 1 else x[b]).astype(F32)
        for layer in range(depth):
            i = layer // 2
            ffn = functools.partial(_ffn, w_gate=wg_all, w_up=wu_all, w_down=wd_all)
            if layer % 2 == 0:
                h, u = ffn(h, ffn_norm[layer, 0], idx=2 * layer,
                           post=(mix_norm[layer], s5_w_in[i]))
                h = _s5_mixer(h, u, s5_a_re[i], s5_a_im[i], s5_log_dt[i], s5_b_re[i],
                              s5_b_im[i], s5_c_re[i], s5_c_im[i], s5_d[i], s5_w_glu[i],
                              s5_w_out[i])
                h = ffn(h, ffn_norm[layer, 1], idx=2 * layer + 1)
            else:
                h = ffn(h, ffn_norm[layer, 0], idx=2 * layer)
                o = _moba_heads(h, mix_norm[layer], moba_w_qkv[i], moba_q_norm[i],
                                moba_k_norm[i])
                h = ffn(h, ffn_norm[layer, 1], idx=2 * layer + 1, pre=(o, moba_w_out[i]))
        outs.append(h)
    out = outs[0].reshape(1, seq, d) if bsz == 1 else jnp.stack(outs)
    return out.astype(x.dtype)
```

```python
import functools

import jax
import jax.numpy as jnp
from jax import lax
from jax.experimental import pallas as pl
from jax.experimental.pallas import tpu as pltpu

F32 = jnp.float32
BF16 = jnp.bfloat16

LANES = 128
SUBLANES = 8
VMEM_LIMIT_BYTES = 56 * 1024 * 1024

EPS = 1e-6
MACARON_WEIGHT = 0.5
S5_GROUP = 16
S5_STATE = 64
GROUPS_PER_SLAB = LANES // S5_GROUP
SLAB_STATES = GROUPS_PER_SLAB * S5_STATE
N_HEADS = 8
HEAD_DIM = 128
MOBA_BLOCK = 256
MOBA_TOPK = 3
ROPE_THETA = 500000.0
ROT_DIM = HEAD_DIM // 4
ROT_HALF = ROT_DIM // 2

FFN_ROWS = 512
FFN_CHUNK = 256
PROJ_ROWS = 512
SSM_ROWS = 2048
ATTN_TILE_BLOCKS = 2
ATTN_GROUP = 8
LOG2_E = 1.4426950408889634


def _params(*semantics):
    return pltpu.CompilerParams(dimension_semantics=semantics,
                                vmem_limit_bytes=VMEM_LIMIT_BYTES)


def _rms_normed(x, gain):
    ms = jnp.mean(x * x, axis=-1, keepdims=True)
    return x * lax.rsqrt(ms + EPS) * gain


def _ffn_body(*refs, has_pre, has_post):
    refs = list(refs)
    x_ref, g_ref, wg_ref, wu_ref, wd_ref = refs[:5]
    rest = refs[5:]
    if has_pre:
        a_ref, wpre_ref = rest[:2]
        rest = rest[2:]
    if has_post:
        gpost_ref, wpost_ref = rest[:2]
        rest = rest[2:]
    o_ref = rest[0]
    act_ref = rest[2 if has_post else 1]

    x = x_ref[...]
    if has_pre:
        x = x + jnp.dot(a_ref[...], wpre_ref[...], preferred_element_type=F32)
    xn = _rms_normed(x, g_ref[...]).astype(BF16)
    d_ff = wg_ref.shape[1]
    for j in range(d_ff // FFN_CHUNK):
        cols = slice(j * FFN_CHUNK, (j + 1) * FFN_CHUNK)
        gate = jnp.dot(xn, wg_ref[:, cols].astype(BF16), preferred_element_type=F32)
        up = jnp.dot(xn, wu_ref[:, cols].astype(BF16), preferred_element_type=F32)
        act_ref[:, cols] = (gate * jax.nn.sigmoid(gate) * up).astype(BF16)
    y = jnp.dot(act_ref[...], wd_ref[...].astype(BF16), preferred_element_type=F32)
    out = x + MACARON_WEIGHT * y
    o_ref[...] = out
    if has_post:
        nxt_ref, tm_ref = rest[1], rest[3]
        outn = _rms_normed(out, gpost_ref[...]).astype(BF16)
        nxt = jnp.dot(outn, wpost_ref[...], preferred_element_type=F32)
        n_chunks = nxt.shape[0] // SUBLANES
        for s in range(nxt_ref.shape[0]):
            tm_ref[s] = nxt[:, s * LANES:(s + 1) * LANES]
        for s in range(nxt_ref.shape[0]):
            for k in range(SUBLANES):
                nxt_ref[s, :, k * LANES:(k + 1) * LANES] = (
                    tm_ref[s, pl.ds(k, n_chunks, stride=SUBLANES), :])


def _ffn(h, gain, w_gate, w_up, w_down, idx, pre=None, post=None):
    seq, d = h.shape
    d_ff = w_gate.shape[2]
    rows = lambda n: pl.BlockSpec((FFN_ROWS, n), lambda i: (i, 0))
    once = lambda shape: pl.BlockSpec(shape, lambda i: (0,) * len(shape),
                                      pipeline_mode=pl.Buffered(1))
    stacked = lambda a, b: pl.BlockSpec((None, a, b), lambda i: (idx, 0, 0),
                                        pipeline_mode=pl.Buffered(1))
    args = [h, gain.reshape(1, d), w_gate, w_up, w_down]
    in_specs = [rows(d), once((1, d)), stacked(d, d_ff), stacked(d, d_ff), stacked(d_ff, d)]
    if pre is not None:
        a, w_pre = pre
        args += [a, w_pre.astype(BF16)]
        in_specs += [rows(a.shape[1]), once(w_pre.shape)]
    out_shape = jax.ShapeDtypeStruct((seq, d), F32)
    out_specs = rows(d)
    scratch = [pltpu.VMEM((FFN_ROWS, d_ff), BF16)]
    if post is not None:
        g_post, w_post = post
        args += [g_post.reshape(1, d), w_post.astype(BF16)]
        in_specs += [once((1, d)), once(w_post.shape)]
        n_slabs = w_post.shape[1] // LANES
        wide = SUBLANES * LANES
        out_shape = (out_shape, jax.ShapeDtypeStruct((n_slabs, seq // SUBLANES, wide), F32))
        out_specs = (out_specs, pl.BlockSpec((n_slabs, FFN_ROWS // SUBLANES, wide),
                                             lambda i: (0, i, 0)))
        scratch.append(pltpu.VMEM((n_slabs, FFN_ROWS, LANES), F32))
    return pl.pallas_call(
        functools.partial(_ffn_body, has_pre=pre is not None, has_post=post is not None),
        out_shape=out_shape,
        grid=(seq // FFN_ROWS,),
        in_specs=in_specs,
        out_specs=out_specs,
        scratch_shapes=scratch,
        compiler_params=_params("arbitrary"),
        name="ffn",
    )(*args)


def _s5_prep_body(are_ref, aim_ref, ldt_ref, bre_ref, bim_ref,
                  pre_ref, pim_ref, bbre_ref, bbim_ref):
    a_re = are_ref[...]
    a_im = aim_ref[...]
    dt = jnp.exp(ldt_ref[...])
    mag = jnp.exp(a_re * dt)
    lam_re = mag * jnp.cos(a_im * dt)
    lam_im = mag * jnp.sin(a_im * dt)
    den = a_re * a_re + a_im * a_im
    num_re = lam_re - 1.0
    coef_re = (num_re * a_re + lam_im * a_im) / den
    coef_im = (lam_im * a_re - num_re * a_im) / den
    b_re = bre_ref[...]
    b_im = bim_ref[...]
    bbre_ref[...] = coef_re * b_re - coef_im * b_im
    bbim_ref[...] = coef_re * b_im + coef_im * b_re
    p_re, p_im = lam_re, lam_im
    for r in range(SUBLANES):
        pre_ref[r:r + 1, :] = p_re
        pim_ref[r:r + 1, :] = p_im
        p_re, p_im = p_re * lam_re - p_im * lam_im, p_re * lam_im + p_im * lam_re


def _s5_prep(a_re, a_im, log_dt, b_re, b_im):
    groups, states = a_re.shape
    n_flat = groups * states
    row = lambda a: a.astype(F32).reshape(1, n_flat)
    ldt = jnp.broadcast_to(log_dt.astype(F32)[:, None], (groups, states)).reshape(1, n_flat)
    chan_rows = lambda b: b.astype(F32).transpose(2, 0, 1).reshape(S5_GROUP, n_flat)
    return pl.pallas_call(
        _s5_prep_body,
        out_shape=(
            jax.ShapeDtypeStruct((SUBLANES, n_flat), F32),
            jax.ShapeDtypeStruct((SUBLANES, n_flat), F32),
            jax.ShapeDtypeStruct((S5_GROUP, n_flat), F32),
            jax.ShapeDtypeStruct((S5_GROUP, n_flat), F32),
        ),
        name="s5_prep",
    )(row(a_re), row(a_im), ldt, chan_rows(b_re), chan_rows(b_im))


def _s5_ops_body(wb_ref, wc_ref, wct_ref, pre_ref, pim_ref,
                 ktoe_ref, wg_ref, wet_ref, p8re_ref, p8im_ref):
    ns = SLAB_STATES
    b_re, b_im = wb_ref[0, :, 0:ns], wb_ref[0, :, ns:2 * ns]
    c_re, c_im = wc_ref[0, 0:ns, :], wc_ref[0, ns:2 * ns, :]
    ct_re, ct_im = wct_ref[0, :, 0:ns], wct_ref[0, :, ns:2 * ns]
    powers = [(jnp.ones((1, ns), F32), jnp.zeros((1, ns), F32))]
    powers += [(pre_ref[0, r:r + 1, :], pim_ref[0, r:r + 1, :]) for r in range(SUBLANES)]
    hi = lax.Precision.HIGHEST
    for k in range(SUBLANES):
        for j in range(k):
            ktoe_ref[0, k * LANES:(k + 1) * LANES, j * LANES:(j + 1) * LANES] = (
                jnp.zeros((LANES, LANES), ktoe_ref.dtype))
    for t in range(SUBLANES):
        l_re, l_im = powers[t]
        lb_re = b_re * l_re - b_im * l_im
        lb_im = b_re * l_im + b_im * l_re
        k_t = (jnp.dot(lb_re, c_re, precision=hi, preferred_element_type=F32)
               - jnp.dot(lb_im, c_im, precision=hi, preferred_element_type=F32))
        for k in range(SUBLANES - t):
            j = k + t
            ktoe_ref[0, k * LANES:(k + 1) * LANES, j * LANES:(j + 1) * LANES] = (
                k_t.astype(ktoe_ref.dtype))
        k = SUBLANES - 1 - t
        wg_ref[0, k * LANES:(k + 1) * LANES, 0:ns] = lb_re.astype(wg_ref.dtype)
        wg_ref[0, k * LANES:(k + 1) * LANES, ns:2 * ns] = lb_im.astype(wg_ref.dtype)
    for k in range(SUBLANES):
        l_re, l_im = powers[k + 1]
        wet_ref[0, k * LANES:(k + 1) * LANES, 0:ns] = (ct_re * l_re - ct_im * l_im).astype(wet_ref.dtype)
        wet_ref[0, k * LANES:(k + 1) * LANES, ns:2 * ns] = (-(ct_re * l_im + ct_im * l_re)).astype(wet_ref.dtype)
    m_re, m_im = powers[SUBLANES]
    q_re, q_im = m_re, m_im
    for r in range(SUBLANES):
        p8re_ref[0, r:r + 1, :] = q_re
        p8im_ref[0, r:r + 1, :] = q_im
        q_re, q_im = q_re * m_re - q_im * m_im, q_re * m_im + q_im * m_re


def _s5_ops(wb, wc, wct, pow_re, pow_im):
    n_slabs = wb.shape[0]
    ns = SLAB_STATES
    wide = SUBLANES * LANES
    slab = lambda *shape: pl.BlockSpec((1,) + shape, lambda j: (j, 0, 0))
    return pl.pallas_call(
        _s5_ops_body,
        out_shape=(
            jax.ShapeDtypeStruct((n_slabs, wide, wide), BF16),
            jax.ShapeDtypeStruct((n_slabs, wide, 2 * ns), BF16),
            jax.ShapeDtypeStruct((n_slabs, wide, 2 * ns), BF16),
            jax.ShapeDtypeStruct((n_slabs, SUBLANES, ns), F32),
            jax.ShapeDtypeStruct((n_slabs, SUBLANES, ns), F32),
        ),
        grid=(n_slabs,),
        in_specs=[slab(LANES, 2 * ns), slab(2 * ns, LANES), slab(LANES, 2 * ns),
                  slab(SUBLANES, ns), slab(SUBLANES, ns)],
        out_specs=(slab(wide, wide), slab(wide, 2 * ns), slab(wide, 2 * ns),
                   slab(SUBLANES, ns), slab(SUBLANES, ns)),
        compiler_params=_params("arbitrary"),
        name="s5_ops",
    )(wb, wc, wct, pow_re, pow_im)


def _s5_ssm_body(u_ref, uold_ref, ktoe_ref, wg_ref, wet_ref, pre_ref, pim_ref, y_ref,
                 g_ref, h_ref, carry_ref):
    t_idx = pl.program_id(1)
    ns = SLAB_STATES
    n_chunks = u_ref.shape[0]
    slot = t_idx % 2

    @pl.when(t_idx == 0)
    def _():
        carry_ref[...] = jnp.zeros_like(carry_ref)
        g_ref[...] = jnp.zeros_like(g_ref)
        h_ref[...] = jnp.zeros_like(h_ref)

    h_in = h_ref[slot, pl.ds(SUBLANES - 1, n_chunks), :].astype(BF16)
    y_ref[...] = (
        jnp.dot(uold_ref[...].astype(BF16), ktoe_ref[0], preferred_element_type=F32)
        + lax.dot_general(h_in, wet_ref[0], (((1,), (1,)), ((), ())),
                          preferred_element_type=F32))

    p_re = pre_ref[0]
    p_im = pim_ref[0]
    srow = lax.broadcasted_iota(jnp.int32, (SUBLANES, ns), 0)
    steps = []
    for s in (1, 2, 4):
        keep = srow >= s
        steps.append((s,
                      jnp.where(keep, p_re[s - 1:s, :], 0.0),
                      jnp.where(keep, p_im[s - 1:s, :], 0.0)))
    g_old = g_ref.at[1 - slot]
    h_new = h_ref.at[1 - slot]
    h_new[0:SUBLANES, :] = carry_ref[...]
    c_re, c_im = carry_ref[:, 0:ns], carry_ref[:, ns:2 * ns]
    last = SUBLANES - 1
    for i in range(n_chunks // SUBLANES):
        r0 = i * SUBLANES
        x_re = g_old[r0:r0 + SUBLANES, 0:ns]
        x_im = g_old[r0:r0 + SUBLANES, ns:2 * ns]
        for s, a_re, a_im in steps:
            s_re = pltpu.roll(x_re, s, 0)
            s_im = pltpu.roll(x_im, s, 0)
            x_re, x_im = (x_re + (a_re * s_re - a_im * s_im),
                          x_im + (a_re * s_im + a_im * s_re))
        x_re, x_im = (x_re + (p_re * c_re - p_im * c_im),
                      x_im + (p_re * c_im + p_im * c_re))
        h_new[r0 + SUBLANES:r0 + 2 * SUBLANES, 0:ns] = x_re
        h_new[r0 + SUBLANES:r0 + 2 * SUBLANES, ns:2 * ns] = x_im
        c_re = jnp.broadcast_to(x_re[last:last + 1, :], (SUBLANES, ns))
        c_im = jnp.broadcast_to(x_im[last:last + 1, :], (SUBLANES, ns))
    carry_ref[:, 0:ns] = c_re
    carry_ref[:, ns:2 * ns] = c_im

    g_ref[slot] = jnp.dot(u_ref[...].astype(BF16), wg_ref[0], preferred_element_type=F32)


def _s5_ssm(u, ktoe, wg, wet, pow8_re, pow8_im):
    n_slabs, total_chunks, wide = u.shape
    ns = SLAB_STATES
    n_chunks = min(SSM_ROWS // SUBLANES, total_chunks)
    n_tiles = total_chunks // n_chunks
    slab = lambda *shape: pl.BlockSpec((1,) + shape, lambda b, t: (b, 0, 0))
    return pl.pallas_call(
        _s5_ssm_body,
        out_shape=jax.ShapeDtypeStruct((n_slabs, total_chunks, wide), F32),
        grid=(n_slabs, n_tiles + 2),
        in_specs=[
            pl.BlockSpec((None, n_chunks, wide), lambda b, t: (b, jnp.minimum(t, n_tiles - 1), 0)),
            pl.BlockSpec((None, n_chunks, wide), lambda b, t: (b, jnp.maximum(t - 2, 0), 0)),
            slab(wide, wide), slab(wide, 2 * ns), slab(wide, 2 * ns),
            slab(SUBLANES, ns), slab(SUBLANES, ns),
        ],
        out_specs=pl.BlockSpec((None, n_chunks, wide), lambda b, t: (b, jnp.maximum(t - 2, 0), 0)),
        scratch_shapes=[
            pltpu.VMEM((2, n_chunks, 2 * ns), F32),
            pltpu.VMEM((2, n_chunks + SUBLANES, 2 * ns), F32),
            pltpu.VMEM((SUBLANES, 2 * ns), F32),
        ],
        compiler_params=_params("arbitrary", "arbitrary"),
        name="s5_ssm",
    )(u, u, ktoe, wg, wet, pow8_re, pow8_im)


def _slab_block_diag(w):
    groups, a, b = w.shape
    n_slabs = groups // GROUPS_PER_SLAB
    eye = jnp.eye(GROUPS_PER_SLAB, dtype=w.dtype)
    w4 = w.reshape(n_slabs, GROUPS_PER_SLAB, a, b)
    out = jnp.einsum('jgab,gh->jgahb', w4, eye)
    return out.reshape(n_slabs, GROUPS_PER_SLAB * a, GROUPS_PER_SLAB * b)


def _s5_out_body(h_ref, y_ref, u_ref, d_ref, wglu_ref, wout_ref, o_ref, g_ref, gate_ref,
                 tm_ref):
    step = pl.program_id(0)
    slot = step % 2
    n_slabs, n_chunks, _ = y_ref.shape

    @pl.when(step == 0)
    def _():
        g_ref[...] = jnp.zeros_like(g_ref)
        gate_ref[...] = jnp.zeros_like(gate_ref)

    g_old = g_ref[1 - slot]
    z = g_old * jax.nn.sigmoid(gate_ref[1 - slot])
    o_ref[...] = h_ref[...] + jnp.dot(z.astype(BF16), wout_ref[...], preferred_element_type=F32)

    for s in range(n_slabs):
        gs = jax.nn.gelu(y_ref[s] + d_ref[s] * u_ref[s])
        for k in range(SUBLANES):
            tm_ref[s, pl.ds(k, n_chunks, stride=SUBLANES), :] = gs[:, k * LANES:(k + 1) * LANES]
    g = jnp.concatenate([tm_ref[s] for s in range(n_slabs)], axis=1)
    g_ref[slot] = g
    gate_ref[slot] = jnp.dot(g.astype(BF16), wglu_ref[...], preferred_element_type=F32)


def _s5_out(h, y, u, d_skip, w_glu, w_out):
    seq, d = h.shape
    n_slabs, _, wide = u.shape
    width = n_slabs * LANES
    n_tiles = seq // PROJ_ROWS
    prev_rows = pl.BlockSpec((PROJ_ROWS, d), lambda i: (jnp.maximum(i - 1, 0), 0))
    cur_slabs = pl.BlockSpec((n_slabs, PROJ_ROWS // SUBLANES, wide),
                             lambda i: (0, jnp.minimum(i, n_tiles - 1), 0))
    once = lambda shape: pl.BlockSpec(shape, lambda i: (0,) * len(shape),
                                      pipeline_mode=pl.Buffered(1))
    d_cm = jnp.tile(d_skip.astype(F32).reshape(n_slabs, 1, LANES), (1, 1, SUBLANES))
    return pl.pallas_call(
        _s5_out_body,
        out_shape=jax.ShapeDtypeStruct((seq, d), F32),
        grid=(n_tiles + 1,),
        in_specs=[prev_rows, cur_slabs, cur_slabs,
                  once((n_slabs, 1, wide)), once((width, width)), once((width, d))],
        out_specs=prev_rows,
        scratch_shapes=[pltpu.VMEM((2, PROJ_ROWS, width), F32),
                        pltpu.VMEM((2, PROJ_ROWS, width), F32),
                        pltpu.VMEM((n_slabs, PROJ_ROWS, LANES), F32)],
        compiler_params=_params("arbitrary"),
        name="s5_out",
    )(h, y, u, d_cm, w_glu.astype(BF16), w_out.astype(BF16))


def _s5_mixer(h, u, a_re, a_im, log_dt, b_re, b_im, c_re, c_im, d_skip, w_glu, w_out):
    groups, states = a_re.shape
    pow_re, pow_im, bb_re, bb_im = _s5_prep(a_re, a_im, log_dt, b_re, b_im)
    n_slabs = groups // GROUPS_PER_SLAB
    to_slabs = lambda p: p.reshape(SUBLANES, n_slabs, SLAB_STATES).transpose(1, 0, 2)
    per_group_b = lambda b: b.reshape(S5_GROUP, groups, states).transpose(1, 0, 2)
    wb = jnp.concatenate([_slab_block_diag(per_group_b(bb_re)),
                          _slab_block_diag(per_group_b(bb_im))], axis=-1)
    wct = jnp.concatenate([_slab_block_diag(c_re.astype(F32)),
                           _slab_block_diag(c_im.astype(F32))], axis=-1)
    wc = wct.transpose(0, 2, 1)
    ktoe, wg, wet, pow8_re, pow8_im = _s5_ops(wb, wc, wct, to_slabs(pow_re), to_slabs(pow_im))
    y = _s5_ssm(u, ktoe, wg, wet, pow8_re, pow8_im)
    return _s5_out(h, y, u, d_skip, w_glu, w_out)


def _head_dim_order():
    idx = list(range(ROT_HALF)) + list(range(ROT_DIM, HEAD_DIM // 2 + ROT_HALF))
    idx += list(range(ROT_HALF, ROT_DIM)) + list(range(HEAD_DIM // 2 + ROT_HALF, HEAD_DIM))
    return jnp.asarray(idx, jnp.int32)


def _rope(x, cos, sin_signed):
    return x * cos + pltpu.roll(x, HEAD_DIM // 2, 1) * sin_signed


def _qkv_body(x_ref, g_ref, w_ref, wvt_ref, qg_ref, kg_ref, invf_ref, q_ref, k_ref, vt_ref,
              cos_rel_ref, sin_rel_ref, y_ref):
    step = pl.program_id(0)
    rows = x_ref.shape[0]
    width = q_ref.shape[1]

    @pl.when(step == 0)
    def _():
        rel = lax.broadcasted_iota(jnp.int32, (rows, HEAD_DIM), 0).astype(F32) * invf_ref[...]
        cos_rel_ref[...] = jnp.cos(rel)
        sin_rel_ref[...] = jnp.sin(rel)
        y_ref[...] = jnp.zeros_like(y_ref)

    base = (jnp.maximum(step - 1, 0) * rows).astype(F32) * invf_ref[...]
    cos0, sin0 = jnp.cos(base), jnp.sin(base)
    cos = cos0 * cos_rel_ref[...] - sin0 * sin_rel_ref[...]
    sin = sin0 * cos_rel_ref[...] + cos0 * sin_rel_ref[...]
    lane = lax.broadcasted_iota(jnp.int32, (rows, HEAD_DIM), 1)
    half = HEAD_DIM // 2
    sin_signed = jnp.where(lane < ROT_HALF, -sin,
                           jnp.where((lane >= half) & (lane < half + ROT_HALF), sin, 0.0))

    def finish(part, gain, scale, out_ref):
        for hd in range(width // HEAD_DIM):
            cols = slice(hd * HEAD_DIM, (hd + 1) * HEAD_DIM)
            yh = y_ref[:, part * width + hd * HEAD_DIM:part * width + (hd + 1) * HEAD_DIM]
            yh = _rope(_rms_normed(yh, gain), cos, sin_signed)
            if scale is not None:
                yh = yh * scale
            out_ref[:, cols] = yh.astype(out_ref.dtype)

    finish(0, qg_ref[...], HEAD_DIM ** -0.5 * LOG2_E, q_ref)
    finish(1, kg_ref[...], None, k_ref)

    xn = _rms_normed(x_ref[...], g_ref[...]).astype(BF16)
    vt = lax.dot_general(wvt_ref[...], xn, (((1,), (1,)), ((), ())), preferred_element_type=F32)
    vt_ref[...] = vt.astype(vt_ref.dtype)
    y_ref[...] = jnp.dot(xn, w_ref[...], preferred_element_type=F32)


def _qkv(h, gain, w_qkv, q_gain, k_gain):
    seq, d = h.shape
    width = w_qkv.shape[1] // 3
    inv_freq = ROPE_THETA ** (-jnp.arange(0, ROT_DIM, 2, dtype=F32) / ROT_DIM)
    gap = jnp.zeros((HEAD_DIM // 2 - ROT_HALF,), F32)
    invf = jnp.concatenate([inv_freq, gap, inv_freq, gap]).reshape(1, HEAD_DIM)
    out = jax.ShapeDtypeStruct((seq, width), BF16)
    n_tiles = seq // PROJ_ROWS
    cur_rows = lambda n: pl.BlockSpec((PROJ_ROWS, n), lambda i: (jnp.minimum(i, n_tiles - 1), 0))
    prev_rows = lambda n: pl.BlockSpec((PROJ_ROWS, n), lambda i: (jnp.maximum(i - 1, 0), 0))
    const = lambda shape: pl.BlockSpec(shape, lambda i: (0, 0))
    order = _head_dim_order()
    cols = (jnp.arange(2 * width // HEAD_DIM, dtype=jnp.int32)[:, None] * HEAD_DIM
            + order[None, :]).reshape(-1)
    w_qk = w_qkv[:, :2 * width][:, cols].astype(BF16)
    w_vt = w_qkv[:, 2 * width:].T.astype(BF16)
    q_gain, k_gain = q_gain[order], k_gain[order]
    return pl.pallas_call(
        _qkv_body,
        out_shape=(out, out, jax.ShapeDtypeStruct((width, seq), BF16)),
        grid=(n_tiles + 1,),
        in_specs=[cur_rows(d), const((1, d)), const((d, 2 * width)), const((width, d)),
                  const((1, HEAD_DIM)), const((1, HEAD_DIM)), const((1, HEAD_DIM))],
        out_specs=(prev_rows(width), prev_rows(width),
                   pl.BlockSpec((width, PROJ_ROWS), lambda i: (0, jnp.minimum(i, n_tiles - 1)))),
        scratch_shapes=[pltpu.VMEM((PROJ_ROWS, HEAD_DIM), F32),
                        pltpu.VMEM((PROJ_ROWS, HEAD_DIM), F32),
                        pltpu.VMEM((PROJ_ROWS, 2 * width), F32)],
        compiler_params=_params("arbitrary"),
        name="moba_qkv",
    )(h, gain.reshape(1, d), w_qk, w_vt,
      q_gain.astype(F32).reshape(1, HEAD_DIM), k_gain.astype(F32).reshape(1, HEAD_DIM), invf)


def _attn_body(q_ref, k_ref, vt_ref, o_ref, kmean_ref, bias_ref, s_ref, mx_ref, p_ref,
               acc_ref, l_ref, alpha_ref):
    step = pl.program_id(1)
    n_tiles = pl.num_programs(1) - 1
    tile = jnp.minimum(step, n_tiles - 1)
    n_blocks = k_ref.shape[0] // MOBA_BLOCK
    n_q = q_ref.shape[0]
    top_block = tile * ATTN_TILE_BLOCKS + (ATTN_TILE_BLOCKS - 1)
    neg_inf = -jnp.inf

    @pl.when((pl.program_id(0) == 0) & (step == 0))
    def _():
        p_ref[...] = jnp.zeros(p_ref.shape, BF16)
        acc_ref[...] = jnp.zeros(acc_ref.shape, F32)
        l_ref[...] = jnp.ones(l_ref.shape, F32)
        alpha_ref[...] = jnp.ones(alpha_ref.shape, F32)

    @pl.when(step == 0)
    def _():
        for b in range(n_blocks):
            kb = k_ref[b * MOBA_BLOCK:(b + 1) * MOBA_BLOCK, :].astype(F32)
            kmean_ref[b:b + 1, :] = jnp.mean(kb, axis=0, keepdims=True)

    q = q_ref[...]
    nt = (((1,), (1,)), ((), ()))

    def groups_of(top):
        return (top + ATTN_GROUP) // ATTN_GROUP

    def block_of(group, g, top=top_block):
        return top - (group * ATTN_GROUP + g)

    def scores(block):
        kb0 = pl.multiple_of(jnp.maximum(block, 0) * MOBA_BLOCK, MOBA_BLOCK)
        return lax.dot_general(k_ref[pl.ds(kb0, MOBA_BLOCK), :], q, nt,
                               preferred_element_type=F32)

    def stage_scores(group, causal):
        for g in range(ATTN_GROUP):
            s = scores(block_of(group, g))
            if causal and g < ATTN_TILE_BLOCKS:
                k_pos = lax.broadcasted_iota(jnp.int32, s.shape, 0)
                q_pos = lax.broadcasted_iota(jnp.int32, s.shape, 1)
                q_rel = q_pos - (ATTN_TILE_BLOCKS - 1 - g) * MOBA_BLOCK
                s = jnp.where((q_rel >= MOBA_BLOCK) | (k_pos <= q_rel), s, neg_inf)
            s_ref[g] = s
            mx_ref[g:g + 1, :] = jnp.max(s, axis=0, keepdims=True)

    def stage_values(group, acc, alpha, top=top_block, weight=None):
        pv = None
        for g in range(ATTN_GROUP):
            b = jnp.clip(block_of(group, g, top), 0, n_blocks - 1)
            kb0 = pl.multiple_of(b * MOBA_BLOCK, MOBA_BLOCK)
            d = jnp.dot(vt_ref[:, pl.ds(kb0, MOBA_BLOCK)], p_ref[g], preferred_element_type=F32)
            pv = d if pv is None else pv + d
        if weight is not None:
            pv = pv * weight
        return alpha * acc + pv

    prev_top = top_block - ATTN_TILE_BLOCKS * jnp.where(step == n_tiles, 0, 1)
    prev_acc = stage_values(groups_of(prev_top) - 1, acc_ref[...], alpha_ref[...], prev_top)
    o_ref[...] = (prev_acc / l_ref[...]).T.astype(o_ref.dtype)

    stage_scores(0, causal=True)

    gate = lax.dot_general(kmean_ref[...].astype(BF16), q, nt, preferred_element_type=F32)
    blk = lax.broadcasted_iota(jnp.int32, gate.shape, 0)
    q_lane = lax.broadcasted_iota(jnp.int32, gate.shape, 1)
    own = tile * ATTN_TILE_BLOCKS + sum(
        (q_lane >= j * MOBA_BLOCK).astype(jnp.int32) for j in range(1, ATTN_TILE_BLOCKS))
    blk_f = blk.astype(F32)
    gate = jnp.where(blk < own, gate, neg_inf)
    selected = blk == own
    for j in range(min(MOBA_TOPK, n_blocks)):
        top = jnp.max(gate, axis=0, keepdims=True)
        first = jnp.min(jnp.where(gate == top, blk_f, float(n_blocks)), axis=0, keepdims=True)
        hit = blk_f == first
        selected = selected | (hit & (own > j))
        gate = jnp.where(hit, neg_inf, gate)
    bias_ref[0:n_blocks, :] = jnp.where(selected, 0.0, neg_inf)
    bias_ref[n_blocks:, :] = jnp.full((bias_ref.shape[0] - n_blocks, n_q), neg_inf, F32)

    def bias_row(group, g):
        b = block_of(group, g)
        return bias_ref[pl.ds(jnp.where(b < 0, n_blocks, b), 1), :]

    def trip(i, carry):
        m, l, acc, alpha_prev = carry
        acc = stage_values(i - 1, acc, alpha_prev, weight=jnp.where(i > 0, 1.0, 0.0))

        rows = [bias_row(i, g) for g in range(ATTN_GROUP)]
        m_new = m
        for g in range(ATTN_GROUP):
            m_new = jnp.maximum(m_new, mx_ref[g:g + 1, :] + rows[g])
        alpha = jnp.exp2(m - m_new)
        l = alpha * l
        for g in range(ATTN_GROUP):
            p = jnp.exp2(s_ref[g] - (m_new - rows[g]))
            l = l + jnp.sum(p, axis=0, keepdims=True)
            p_ref[g] = p.astype(BF16)

        stage_scores(i + 1, causal=False)
        return m_new, l, acc, alpha

    n_groups = jnp.where(step == n_tiles, 0, groups_of(top_block))
    init = (jnp.full((1, n_q), neg_inf, F32), jnp.zeros((1, n_q), F32),
            jnp.zeros((HEAD_DIM, n_q), F32), jnp.ones((1, n_q), F32))
    m, l, acc, alpha = lax.fori_loop(0, n_groups, trip, init)
    acc_ref[...] = acc
    l_ref[...] = jnp.where(l > 0.0, l, 1.0)
    alpha_ref[...] = alpha


def _attention(q, k, vt):
    seq, width = q.shape
    n_heads = width // HEAD_DIM
    n_blocks = seq // MOBA_BLOCK
    n_q = ATTN_TILE_BLOCKS * MOBA_BLOCK
    bias_rows = n_blocks + SUBLANES
    n_tiles = seq // n_q
    return pl.pallas_call(
        _attn_body,
        out_shape=jax.ShapeDtypeStruct((seq, width), BF16),
        grid=(n_heads, n_tiles + 1),
        in_specs=[
            pl.BlockSpec((n_q, HEAD_DIM), lambda hd, t: (jnp.minimum(t, n_tiles - 1), hd)),
            pl.BlockSpec((seq, HEAD_DIM), lambda hd, t: (0, hd)),
            pl.BlockSpec((HEAD_DIM, seq), lambda hd, t: (hd, 0)),
        ],
        out_specs=pl.BlockSpec((n_q, HEAD_DIM), lambda hd, t: (jnp.maximum(t - 1, 0), hd)),
        scratch_shapes=[pltpu.VMEM((n_blocks, HEAD_DIM), F32),
                        pltpu.VMEM((bias_rows, n_q), F32),
                        pltpu.VMEM((ATTN_GROUP, MOBA_BLOCK, n_q), F32),
                        pltpu.VMEM((ATTN_GROUP, n_q), F32),
                        pltpu.VMEM((ATTN_GROUP, MOBA_BLOCK, n_q), BF16),
                        pltpu.VMEM((HEAD_DIM, n_q), F32),
                        pltpu.VMEM((1, n_q), F32),
                        pltpu.VMEM((1, n_q), F32)],
        compiler_params=_params("arbitrary", "arbitrary"),
        name="moba_attn",
    )(q, k, vt)


def _moba_heads(h, gain, w_qkv, q_gain, k_gain):
    q, k, vt = _qkv(h, gain, w_qkv, q_gain, k_gain)
    return _attention(q, k, vt)


def kernel(x, ffn_norm, ffn_w_gate, ffn_w_up, ffn_w_down, mix_norm, s5_w_in, s5_a_re, s5_a_im, s5_log_dt, s5_b_re, s5_b_im, s5_c_re, s5_c_im, s5_d, s5_w_glu, s5_w_out, moba_w_qkv, moba_q_norm, moba_k_norm, moba_w_out):
    bsz, seq, d = x.shape
    depth = ffn_norm.shape[0]
    d_ff = ffn_w_gate.shape[-1]
    assert seq % (ATTN_TILE_BLOCKS * MOBA_BLOCK) == 0 and seq % FFN_ROWS == 0
    assert seq % min(SSM_ROWS, seq) == 0 and seq % PROJ_ROWS == 0
    assert d % LANES == 0 and d_ff % FFN_CHUNK == 0
    assert s5_a_re.shape[1:] == (d // S5_GROUP, S5_STATE)
    assert moba_w_qkv.shape[-1] == 3 * d and d % HEAD_DIM == 0
    wg_all = ffn_w_gate.reshape(depth * 2, d, d_ff)
    wu_all = ffn_w_up.reshape(depth * 2, d, d_ff)
    wd_all = ffn_w_down.reshape(depth * 2, d_ff, d)
    outs = []
    for b in range(bsz):
        h = (x.reshape(seq, d) if bsz == 1 else x[b]).astype(F32)
        for layer in range(depth):
            i = layer // 2
            ffn = functools.partial(_ffn, w_gate=wg_all, w_up=wu_all, w_down=wd_all)
            if layer % 2 == 0:
                h, u = ffn(h, ffn_norm[layer, 0], idx=2 * layer,
                           post=(mix_norm[layer], s5_w_in[i]))
                h = _s5_mixer(h, u, s5_a_re[i], s5_a_im[i], s5_log_dt[i], s5_b_re[i],
                              s5_b_im[i], s5_c_re[i], s5_c_im[i], s5_d[i], s5_w_glu[i],
                              s5_w_out[i])
                h = ffn(h, ffn_norm[layer, 1], idx=2 * layer + 1)
            else:
                h = ffn(h, ffn_norm[layer, 0], idx=2 * layer)
                o = _moba_heads(h, mix_norm[layer], moba_w_qkv[i], moba_q_norm[i],
                                moba_k_norm[i])
                h = ffn(h, ffn_norm[layer, 1], idx=2 * layer + 1, pre=(o, moba_w_out[i]))
        outs.append(h)
    out = outs[0].reshape(1, seq, d) if bsz == 1 else jnp.stack(outs)
    return out.astype(x.dtype)
```

```python
import functools

import jax
import jax.numpy as jnp
from jax import lax
from jax.experimental import pallas as pl
from jax.experimental.pallas import tpu as pltpu

F32 = jnp.float32
BF16 = jnp.bfloat16

LANES = 128
SUBLANES = 8
VMEM_LIMIT_BYTES = 56 * 1024 * 1024

EPS = 1e-6
MACARON_WEIGHT = 0.5
S5_GROUP = 16
S5_STATE = 64
GROUPS_PER_SLAB = LANES // S5_GROUP
SLAB_STATES = GROUPS_PER_SLAB * S5_STATE
HEAD_DIM = 128
MOBA_BLOCK = 256
MOBA_TOPK = 3
ROPE_THETA = 500000.0
ROT_DIM = HEAD_DIM // 4
ROT_HALF = ROT_DIM // 2

FFN_ROWS = 512
FFN_CHUNK = 256
PROJ_ROWS = 512
SSM_ROWS = 2048
ATTN_TILE_BLOCKS = 2
ATTN_GROUP = 8
LOG2_E = 1.4426950408889634


def _params(*semantics):
    return pltpu.CompilerParams(dimension_semantics=semantics,
                                vmem_limit_bytes=VMEM_LIMIT_BYTES)


def _rms_normed(x, gain):
    ms = jnp.mean(x * x, axis=-1, keepdims=True)
    return x * lax.rsqrt(ms + EPS) * gain


def _ffn_body(*refs, has_pre, has_post):
    refs = list(refs)
    x_ref, g_ref, wg_ref, wu_ref, wd_ref = refs[:5]
    rest = refs[5:]
    if has_pre:
        a_ref, wpre_ref = rest[:2]
        rest = rest[2:]
    if has_post:
        gpost_ref, wpost_ref = rest[:2]
        rest = rest[2:]
    o_ref = rest[0]
    act_ref = rest[2 if has_post else 1]

    x = x_ref[...]
    if has_pre:
        x = x + jnp.dot(a_ref[...], wpre_ref[...], preferred_element_type=F32)
    xn = _rms_normed(x, g_ref[...]).astype(BF16)
    d_ff = wg_ref.shape[1]
    for j in range(d_ff // FFN_CHUNK):
        cols = slice(j * FFN_CHUNK, (j + 1) * FFN_CHUNK)
        gate = jnp.dot(xn, wg_ref[:, cols].astype(BF16), preferred_element_type=F32)
        up = jnp.dot(xn, wu_ref[:, cols].astype(BF16), preferred_element_type=F32)
        act_ref[:, cols] = (gate * jax.nn.sigmoid(gate) * up).astype(BF16)
    y = jnp.dot(act_ref[...], wd_ref[...].astype(BF16), preferred_element_type=F32)
    out = x + MACARON_WEIGHT * y
    o_ref[...] = out
    if has_post:
        nxt_ref, tm_ref = rest[1], rest[3]
        outn = _rms_normed(out, gpost_ref[...]).astype(BF16)
        nxt = jnp.dot(outn, wpost_ref[...], preferred_element_type=F32)
        n_chunks = nxt.shape[0] // SUBLANES
        for s in range(nxt_ref.shape[0]):
            tm_ref[s] = nxt[:, s * LANES:(s + 1) * LANES]
        for s in range(nxt_ref.shape[0]):
            for k in range(SUBLANES):
                nxt_ref[s, :, k * LANES:(k + 1) * LANES] = (
                    tm_ref[s, pl.ds(k, n_chunks, stride=SUBLANES), :])


def _ffn(h, gain, w_gate, w_up, w_down, idx, pre=None, post=None):
    seq, d = h.shape
    d_ff = w_gate.shape[2]
    rows = lambda n: pl.BlockSpec((FFN_ROWS, n), lambda i: (i, 0))
    once = lambda shape: pl.BlockSpec(shape, lambda i: (0,) * len(shape),
                                      pipeline_mode=pl.Buffered(1))
    stacked = lambda a, b: pl.BlockSpec((None, a, b), lambda i: (idx, 0, 0),
                                        pipeline_mode=pl.Buffered(1))
    args = [h, gain.reshape(1, d), w_gate, w_up, w_down]
    in_specs = [rows(d), once((1, d)), stacked(d, d_ff), stacked(d, d_ff), stacked(d_ff, d)]
    if pre is not None:
        a, w_pre = pre
        args += [a, w_pre.astype(BF16)]
        in_specs += [rows(a.shape[1]), once(w_pre.shape)]
    out_shape = jax.ShapeDtypeStruct((seq, d), F32)
    out_specs = rows(d)
    scratch = [pltpu.VMEM((FFN_ROWS, d_ff), BF16)]
    if post is not None:
        g_post, w_post = post
        args += [g_post.reshape(1, d), w_post.astype(BF16)]
        in_specs += [once((1, d)), once(w_post.shape)]
        n_slabs = w_post.shape[1] // LANES
        wide = SUBLANES * LANES
        out_shape = (out_shape, jax.ShapeDtypeStruct((n_slabs, seq // SUBLANES, wide), F32))
        out_specs = (out_specs, pl.BlockSpec((n_slabs, FFN_ROWS // SUBLANES, wide),
                                             lambda i: (0, i, 0)))
        scratch.append(pltpu.VMEM((n_slabs, FFN_ROWS, LANES), F32))
    return pl.pallas_call(
        functools.partial(_ffn_body, has_pre=pre is not None, has_post=post is not None),
        out_shape=out_shape,
        grid=(seq // FFN_ROWS,),
        in_specs=in_specs,
        out_specs=out_specs,
        scratch_shapes=scratch,
        compiler_params=_params("arbitrary"),
        name="ffn",
    )(*args)


def _s5_prep_body(are_ref, aim_ref, ldt_ref, bre_ref, bim_ref,
                  pre_ref, pim_ref, bbre_ref, bbim_ref):
    a_re = are_ref[...]
    a_im = aim_ref[...]
    dt = jnp.exp(ldt_ref[...])
    mag = jnp.exp(a_re * dt)
    lam_re = mag * jnp.cos(a_im * dt)
    lam_im = mag * jnp.sin(a_im * dt)
    den = a_re * a_re + a_im * a_im
    num_re = lam_re - 1.0
    coef_re = (num_re * a_re + lam_im * a_im) / den
    coef_im = (lam_im * a_re - num_re * a_im) / den
    b_re = bre_ref[...]
    b_im = bim_ref[...]
    bbre_ref[...] = coef_re * b_re - coef_im * b_im
    bbim_ref[...] = coef_re * b_im + coef_im * b_re
    p_re, p_im = lam_re, lam_im
    for r in range(SUBLANES):
        pre_ref[r:r + 1, :] = p_re
        pim_ref[r:r + 1, :] = p_im
        p_re, p_im = p_re * lam_re - p_im * lam_im, p_re * lam_im + p_im * lam_re


def _s5_prep(a_re, a_im, log_dt, b_re, b_im):
    groups, states = a_re.shape
    n_flat = groups * states
    row = lambda a: a.astype(F32).reshape(1, n_flat)
    ldt = jnp.broadcast_to(log_dt.astype(F32)[:, None], (groups, states)).reshape(1, n_flat)
    chan_rows = lambda b: b.astype(F32).transpose(2, 0, 1).reshape(S5_GROUP, n_flat)
    return pl.pallas_call(
        _s5_prep_body,
        out_shape=(
            jax.ShapeDtypeStruct((SUBLANES, n_flat), F32),
            jax.ShapeDtypeStruct((SUBLANES, n_flat), F32),
            jax.ShapeDtypeStruct((S5_GROUP, n_flat), F32),
            jax.ShapeDtypeStruct((S5_GROUP, n_flat), F32),
        ),
        name="s5_prep",
    )(row(a_re), row(a_im), ldt, chan_rows(b_re), chan_rows(b_im))


def _s5_ops_body(wb_ref, wc_ref, wct_ref, pre_ref, pim_ref,
                 ktoe_ref, wg_ref, wet_ref, p8re_ref, p8im_ref):
    ns = SLAB_STATES
    b_re, b_im = wb_ref[0, :, 0:ns], wb_ref[0, :, ns:2 * ns]
    c_re, c_im = wc_ref[0, 0:ns, :], wc_ref[0, ns:2 * ns, :]
    ct_re, ct_im = wct_ref[0, :, 0:ns], wct_ref[0, :, ns:2 * ns]
    powers = [(jnp.ones((1, ns), F32), jnp.zeros((1, ns), F32))]
    powers += [(pre_ref[0, r:r + 1, :], pim_ref[0, r:r + 1, :]) for r in range(SUBLANES)]
    hi = lax.Precision.HIGHEST
    for k in range(SUBLANES):
        for j in range(k):
            ktoe_ref[0, k * LANES:(k + 1) * LANES, j * LANES:(j + 1) * LANES] = (
                jnp.zeros((LANES, LANES), ktoe_ref.dtype))
    for t in range(SUBLANES):
        l_re, l_im = powers[t]
        lb_re = b_re * l_re - b_im * l_im
        lb_im = b_re * l_im + b_im * l_re
        k_t = (jnp.dot(lb_re, c_re, precision=hi, preferred_element_type=F32)
               - jnp.dot(lb_im, c_im, precision=hi, preferred_element_type=F32))
        for k in range(SUBLANES - t):
            j = k + t
            ktoe_ref[0, k * LANES:(k + 1) * LANES, j * LANES:(j + 1) * LANES] = (
                k_t.astype(ktoe_ref.dtype))
        k = SUBLANES - 1 - t
        wg_ref[0, k * LANES:(k + 1) * LANES, 0:ns] = lb_re.astype(wg_ref.dtype)
        wg_ref[0, k * LANES:(k + 1) * LANES, ns:2 * ns] = lb_im.astype(wg_ref.dtype)
    for k in range(SUBLANES):
        l_re, l_im = powers[k + 1]
        wet_ref[0, k * LANES:(k + 1) * LANES, 0:ns] = (ct_re * l_re - ct_im * l_im).astype(wet_ref.dtype)
        wet_ref[0, k * LANES:(k + 1) * LANES, ns:2 * ns] = (-(ct_re * l_im + ct_im * l_re)).astype(wet_ref.dtype)
    m_re, m_im = powers[SUBLANES]
    q_re, q_im = m_re, m_im
    for r in range(SUBLANES):
        p8re_ref[0, r:r + 1, :] = q_re
        p8im_ref[0, r:r + 1, :] = q_im
        q_re, q_im = q_re * m_re - q_im * m_im, q_re * m_im + q_im * m_re


def _s5_ops(wb, wc, wct, pow_re, pow_im):
    n_slabs = wb.shape[0]
    ns = SLAB_STATES
    wide = SUBLANES * LANES
    slab = lambda *shape: pl.BlockSpec((1,) + shape, lambda j: (j, 0, 0))
    return pl.pallas_call(
        _s5_ops_body,
        out_shape=(
            jax.ShapeDtypeStruct((n_slabs, wide, wide), BF16),
            jax.ShapeDtypeStruct((n_slabs, wide, 2 * ns), BF16),
            jax.ShapeDtypeStruct((n_slabs, wide, 2 * ns), BF16),
            jax.ShapeDtypeStruct((n_slabs, SUBLANES, ns), F32),
            jax.ShapeDtypeStruct((n_slabs, SUBLANES, ns), F32),
        ),
        grid=(n_slabs,),
        in_specs=[slab(LANES, 2 * ns), slab(2 * ns, LANES), slab(LANES, 2 * ns),
                  slab(SUBLANES, ns), slab(SUBLANES, ns)],
        out_specs=(slab(wide, wide), slab(wide, 2 * ns), slab(wide, 2 * ns),
                   slab(SUBLANES, ns), slab(SUBLANES, ns)),
        compiler_params=_params("arbitrary"),
        name="s5_ops",
    )(wb, wc, wct, pow_re, pow_im)


def _s5_ssm_body(u_ref, uold_ref, ktoe_ref, wg_ref, wet_ref, pre_ref, pim_ref, y_ref,
                 g0_ref, g1_ref, h0_ref, h1_ref, carry_ref):
    step = pl.program_id(1)
    ns = SLAB_STATES
    n_chunks = u_ref.shape[0] // 2
    g_refs, h_refs = (g0_ref, g1_ref), (h0_ref, h1_ref)

    @pl.when(step == 0)
    def _():
        carry_ref[...] = jnp.zeros_like(carry_ref)
        for ref in g_refs + h_refs:
            ref[...] = jnp.zeros_like(ref)

    p_re = pre_ref[0]
    p_im = pim_ref[0]
    srow = lax.broadcasted_iota(jnp.int32, (SUBLANES, ns), 0)
    steps = []
    for s in (1, 2, 4):
        keep = srow >= s
        steps.append((s,
                      jnp.where(keep, p_re[s - 1:s, :], 0.0),
                      jnp.where(keep, p_im[s - 1:s, :], 0.0)))
    last = SUBLANES - 1

    for k in range(2):
        rows = slice(k * n_chunks, (k + 1) * n_chunks)
        h_in = h_refs[k][SUBLANES - 1:SUBLANES - 1 + n_chunks, :].astype(BF16)
        y_ref[rows, :] = (
            jnp.dot(uold_ref[rows, :].astype(BF16), ktoe_ref[0], preferred_element_type=F32)
            + lax.dot_general(h_in, wet_ref[0], (((1,), (1,)), ((), ())),
                              preferred_element_type=F32))

        g_old, h_new = g_refs[1 - k], h_refs[1 - k]
        h_new[0:SUBLANES, :] = carry_ref[...]
        c_re, c_im = carry_ref[:, 0:ns], carry_ref[:, ns:2 * ns]
        for i in range(n_chunks // SUBLANES):
            r0 = i * SUBLANES
            x_re = g_old[r0:r0 + SUBLANES, 0:ns]
            x_im = g_old[r0:r0 + SUBLANES, ns:2 * ns]
            for s, a_re, a_im in steps:
                s_re = pltpu.roll(x_re, s, 0)
                s_im = pltpu.roll(x_im, s, 0)
                x_re, x_im = (x_re + (a_re * s_re - a_im * s_im),
                              x_im + (a_re * s_im + a_im * s_re))
            x_re, x_im = (x_re + (p_re * c_re - p_im * c_im),
                          x_im + (p_re * c_im + p_im * c_re))
            h_new[r0 + SUBLANES:r0 + 2 * SUBLANES, 0:ns] = x_re
            h_new[r0 + SUBLANES:r0 + 2 * SUBLANES, ns:2 * ns] = x_im
            c_re = jnp.broadcast_to(x_re[last:last + 1, :], (SUBLANES, ns))
            c_im = jnp.broadcast_to(x_im[last:last + 1, :], (SUBLANES, ns))
        carry_ref[:, 0:ns] = c_re
        carry_ref[:, ns:2 * ns] = c_im

        g_refs[k][...] = jnp.dot(u_ref[rows, :].astype(BF16), wg_ref[0],
                                 preferred_element_type=F32)


def _s5_ssm(u, ktoe, wg, wet, pow8_re, pow8_im):
    n_slabs, total_chunks, wide = u.shape
    ns = SLAB_STATES
    pair = min(2 * SSM_ROWS // SUBLANES, total_chunks)
    n_chunks = pair // 2
    n_steps = total_chunks // pair
    assert pair % (2 * SUBLANES) == 0 and total_chunks % pair == 0
    slab = lambda *shape: pl.BlockSpec((1,) + shape, lambda b, t: (b, 0, 0))
    prev_pair = pl.BlockSpec((None, pair, wide), lambda b, t: (b, jnp.maximum(t - 1, 0), 0))
    return pl.pallas_call(
        _s5_ssm_body,
        out_shape=jax.ShapeDtypeStruct((n_slabs, total_chunks, wide), F32),
        grid=(n_slabs, n_steps + 1),
        in_specs=[
            pl.BlockSpec((None, pair, wide), lambda b, t: (b, jnp.minimum(t, n_steps - 1), 0)),
            prev_pair,
            slab(wide, wide), slab(wide, 2 * ns), slab(wide, 2 * ns),
            slab(SUBLANES, ns), slab(SUBLANES, ns),
        ],
        out_specs=prev_pair,
        scratch_shapes=[
            pltpu.VMEM((n_chunks, 2 * ns), F32),
            pltpu.VMEM((n_chunks, 2 * ns), F32),
            pltpu.VMEM((n_chunks + SUBLANES, 2 * ns), F32),
            pltpu.VMEM((n_chunks + SUBLANES, 2 * ns), F32),
            pltpu.VMEM((SUBLANES, 2 * ns), F32),
        ],
        compiler_params=_params("arbitrary", "arbitrary"),
        name="s5_ssm",
    )(u, u, ktoe, wg, wet, pow8_re, pow8_im)


def _slab_block_diag(w):
    groups, a, b = w.shape
    n_slabs = groups // GROUPS_PER_SLAB
    eye = jnp.eye(GROUPS_PER_SLAB, dtype=w.dtype)
    w4 = w.reshape(n_slabs, GROUPS_PER_SLAB, a, b)
    out = jnp.einsum('jgab,gh->jgahb', w4, eye)
    return out.reshape(n_slabs, GROUPS_PER_SLAB * a, GROUPS_PER_SLAB * b)


def _s5_out_body(h_ref, y_ref, u_ref, d_ref, wglu_ref, wout_ref, o_ref, g_ref, gate_ref,
                 tm_ref):
    step = pl.program_id(0)
    n_slabs, n_chunks, _ = y_ref.shape

    @pl.when(step == 0)
    def _():
        g_ref[...] = jnp.zeros_like(g_ref)
        gate_ref[...] = jnp.zeros_like(gate_ref)

    z = g_ref[...] * jax.nn.sigmoid(gate_ref[...])
    o_ref[...] = h_ref[...] + jnp.dot(z.astype(BF16), wout_ref[...], preferred_element_type=F32)

    for s in range(n_slabs):
        gs = jax.nn.gelu(y_ref[s] + d_ref[s] * u_ref[s])
        for k in range(SUBLANES):
            tm_ref[s, pl.ds(k, n_chunks, stride=SUBLANES), :] = gs[:, k * LANES:(k + 1) * LANES]
    g = jnp.concatenate([tm_ref[s] for s in range(n_slabs)], axis=1)
    g_ref[...] = g
    gate_ref[...] = jnp.dot(g.astype(BF16), wglu_ref[...], preferred_element_type=F32)


def _s5_out(h, y, u, d_skip, w_glu, w_out):
    seq, d = h.shape
    n_slabs, _, wide = u.shape
    width = n_slabs * LANES
    n_tiles = seq // PROJ_ROWS
    prev_rows = pl.BlockSpec((PROJ_ROWS, d), lambda i: (jnp.maximum(i - 1, 0), 0))
    cur_slabs = pl.BlockSpec((n_slabs, PROJ_ROWS // SUBLANES, wide),
                             lambda i: (0, jnp.minimum(i, n_tiles - 1), 0))
    once = lambda shape: pl.BlockSpec(shape, lambda i: (0,) * len(shape),
                                      pipeline_mode=pl.Buffered(1))
    d_cm = jnp.tile(d_skip.astype(F32).reshape(n_slabs, 1, LANES), (1, 1, SUBLANES))
    return pl.pallas_call(
        _s5_out_body,
        out_shape=jax.ShapeDtypeStruct((seq, d), F32),
        grid=(n_tiles + 1,),
        in_specs=[prev_rows, cur_slabs, cur_slabs,
                  once((n_slabs, 1, wide)), once((width, width)), once((width, d))],
        out_specs=prev_rows,
        scratch_shapes=[pltpu.VMEM((PROJ_ROWS, width), F32),
                        pltpu.VMEM((PROJ_ROWS, width), F32),
                        pltpu.VMEM((n_slabs, PROJ_ROWS, LANES), F32)],
        compiler_params=_params("arbitrary"),
        name="s5_out",
    )(h, y, u, d_cm, w_glu.astype(BF16), w_out.astype(BF16))


def _s5_mixer(h, u, a_re, a_im, log_dt, b_re, b_im, c_re, c_im, d_skip, w_glu, w_out):
    groups, states = a_re.shape
    pow_re, pow_im, bb_re, bb_im = _s5_prep(a_re, a_im, log_dt, b_re, b_im)
    n_slabs = groups // GROUPS_PER_SLAB
    to_slabs = lambda p: p.reshape(SUBLANES, n_slabs, SLAB_STATES).transpose(1, 0, 2)
    per_group_b = lambda b: b.reshape(S5_GROUP, groups, states).transpose(1, 0, 2)
    wb = jnp.concatenate([_slab_block_diag(per_group_b(bb_re)),
                          _slab_block_diag(per_group_b(bb_im))], axis=-1)
    wct = jnp.concatenate([_slab_block_diag(c_re.astype(F32)),
                           _slab_block_diag(c_im.astype(F32))], axis=-1)
    wc = wct.transpose(0, 2, 1)
    ktoe, wg, wet, pow8_re, pow8_im = _s5_ops(wb, wc, wct, to_slabs(pow_re), to_slabs(pow_im))
    y = _s5_ssm(u, ktoe, wg, wet, pow8_re, pow8_im)
    return _s5_out(h, y, u, d_skip, w_glu, w_out)


def _head_dim_order():
    idx = list(range(ROT_HALF)) + list(range(ROT_DIM, HEAD_DIM // 2 + ROT_HALF))
    idx += list(range(ROT_HALF, ROT_DIM)) + list(range(HEAD_DIM // 2 + ROT_HALF, HEAD_DIM))
    return jnp.asarray(idx, jnp.int32)


def _rope(x, cos, sin_signed):
    return x * cos + pltpu.roll(x, HEAD_DIM // 2, 1) * sin_signed


def _qkv_body(x_ref, g_ref, w_ref, wvt_ref, qg_ref, kg_ref, invf_ref, q_ref, k_ref, vt_ref,
              cos_rel_ref, sin_rel_ref, y_ref):
    step = pl.program_id(0)
    rows = x_ref.shape[0]
    width = q_ref.shape[1]

    @pl.when(step == 0)
    def _():
        rel = lax.broadcasted_iota(jnp.int32, (rows, HEAD_DIM), 0).astype(F32) * invf_ref[...]
        cos_rel_ref[...] = jnp.cos(rel)
        sin_rel_ref[...] = jnp.sin(rel)
        y_ref[...] = jnp.zeros_like(y_ref)

    base = (jnp.maximum(step - 1, 0) * rows).astype(F32) * invf_ref[...]
    cos0, sin0 = jnp.cos(base), jnp.sin(base)
    cos = cos0 * cos_rel_ref[...] - sin0 * sin_rel_ref[...]
    sin = sin0 * cos_rel_ref[...] + cos0 * sin_rel_ref[...]
    lane = lax.broadcasted_iota(jnp.int32, (rows, HEAD_DIM), 1)
    half = HEAD_DIM // 2
    sin_signed = jnp.where(lane < ROT_HALF, -sin,
                           jnp.where((lane >= half) & (lane < half + ROT_HALF), sin, 0.0))

    def finish(part, gain, scale, out_ref):
        for hd in range(width // HEAD_DIM):
            cols = slice(hd * HEAD_DIM, (hd + 1) * HEAD_DIM)
            yh = y_ref[:, part * width + hd * HEAD_DIM:part * width + (hd + 1) * HEAD_DIM]
            yh = _rope(_rms_normed(yh, gain), cos, sin_signed)
            if scale is not None:
                yh = yh * scale
            out_ref[:, cols] = yh.astype(out_ref.dtype)

    finish(0, qg_ref[...], HEAD_DIM ** -0.5 * LOG2_E, q_ref)
    finish(1, kg_ref[...], None, k_ref)

    xn = _rms_normed(x_ref[...], g_ref[...]).astype(BF16)
    vt = lax.dot_general(wvt_ref[...], xn, (((1,), (1,)), ((), ())), preferred_element_type=F32)
    vt_ref[...] = vt.astype(vt_ref.dtype)
    y_ref[...] = jnp.dot(xn, w_ref[...], preferred_element_type=F32)


def _qkv(h, gain, w_qkv, q_gain, k_gain):
    seq, d = h.shape
    width = w_qkv.shape[1] // 3
    inv_freq = ROPE_THETA ** (-jnp.arange(0, ROT_DIM, 2, dtype=F32) / ROT_DIM)
    gap = jnp.zeros((HEAD_DIM // 2 - ROT_HALF,), F32)
    invf = jnp.concatenate([inv_freq, gap, inv_freq, gap]).reshape(1, HEAD_DIM)
    out = jax.ShapeDtypeStruct((seq, width), BF16)
    n_tiles = seq // PROJ_ROWS
    cur_rows = lambda n: pl.BlockSpec((PROJ_ROWS, n), lambda i: (jnp.minimum(i, n_tiles - 1), 0))
    prev_rows = lambda n: pl.BlockSpec((PROJ_ROWS, n), lambda i: (jnp.maximum(i - 1, 0), 0))
    const = lambda shape: pl.BlockSpec(shape, lambda i: (0, 0))
    order = _head_dim_order()
    cols = (jnp.arange(2 * width // HEAD_DIM, dtype=jnp.int32)[:, None] * HEAD_DIM
            + order[None, :]).reshape(-1)
    w_qk = w_qkv[:, :2 * width][:, cols].astype(BF16)
    w_vt = w_qkv[:, 2 * width:].T.astype(BF16)
    q_gain, k_gain = q_gain[order], k_gain[order]
    return pl.pallas_call(
        _qkv_body,
        out_shape=(out, out, jax.ShapeDtypeStruct((width, seq), BF16)),
        grid=(n_tiles + 1,),
        in_specs=[cur_rows(d), const((1, d)), const((d, 2 * width)), const((width, d)),
                  const((1, HEAD_DIM)), const((1, HEAD_DIM)), const((1, HEAD_DIM))],
        out_specs=(prev_rows(width), prev_rows(width),
                   pl.BlockSpec((width, PROJ_ROWS), lambda i: (0, jnp.minimum(i, n_tiles - 1)))),
        scratch_shapes=[pltpu.VMEM((PROJ_ROWS, HEAD_DIM), F32),
                        pltpu.VMEM((PROJ_ROWS, HEAD_DIM), F32),
                        pltpu.VMEM((PROJ_ROWS, 2 * width), F32)],
        compiler_params=_params("arbitrary"),
        name="moba_qkv",
    )(h, gain.reshape(1, d), w_qk, w_vt,
      q_gain.astype(F32).reshape(1, HEAD_DIM), k_gain.astype(F32).reshape(1, HEAD_DIM), invf)


def _attn_body(q_ref, k_ref, vt_ref, o_ref, kmean_ref, bias_ref, s_ref, mx_ref, p_ref,
               acc_ref, l_ref, alpha_ref):
    step = pl.program_id(1)
    n_tiles = pl.num_programs(1) - 1
    tile = jnp.minimum(step, n_tiles - 1)
    n_blocks = k_ref.shape[0] // MOBA_BLOCK
    n_q = q_ref.shape[0]
    top_block = tile * ATTN_TILE_BLOCKS + (ATTN_TILE_BLOCKS - 1)
    neg_inf = -jnp.inf

    @pl.when((pl.program_id(0) == 0) & (step == 0))
    def _():
        p_ref[...] = jnp.zeros(p_ref.shape, BF16)
        acc_ref[...] = jnp.zeros(acc_ref.shape, F32)
        l_ref[...] = jnp.ones(l_ref.shape, F32)
        alpha_ref[...] = jnp.ones(alpha_ref.shape, F32)

    @pl.when(step == 0)
    def _():
        for b in range(n_blocks):
            kb = k_ref[b * MOBA_BLOCK:(b + 1) * MOBA_BLOCK, :].astype(F32)
            kmean_ref[b:b + 1, :] = jnp.mean(kb, axis=0, keepdims=True)

    q = q_ref[...]
    nt = (((1,), (1,)), ((), ()))

    def groups_of(top):
        return (top + ATTN_GROUP) // ATTN_GROUP

    def block_of(group, g, top=top_block):
        return top - (group * ATTN_GROUP + g)

    def scores(block):
        kb0 = pl.multiple_of(jnp.maximum(block, 0) * MOBA_BLOCK, MOBA_BLOCK)
        return lax.dot_general(k_ref[pl.ds(kb0, MOBA_BLOCK), :], q, nt,
                               preferred_element_type=F32)

    def stage_scores(group, causal):
        for g in range(ATTN_GROUP):
            s = scores(block_of(group, g))
            if causal and g < ATTN_TILE_BLOCKS:
                k_pos = lax.broadcasted_iota(jnp.int32, s.shape, 0)
                q_pos = lax.broadcasted_iota(jnp.int32, s.shape, 1)
                q_rel = q_pos - (ATTN_TILE_BLOCKS - 1 - g) * MOBA_BLOCK
                s = jnp.where((q_rel >= MOBA_BLOCK) | (k_pos <= q_rel), s, neg_inf)
            s_ref[g] = s
            mx_ref[g:g + 1, :] = jnp.max(s, axis=0, keepdims=True)

    def stage_values(group, acc, alpha, top=top_block, weight=None):
        pv = None
        for g in range(ATTN_GROUP):
            b = jnp.clip(block_of(group, g, top), 0, n_blocks - 1)
            kb0 = pl.multiple_of(b * MOBA_BLOCK, MOBA_BLOCK)
            d = jnp.dot(vt_ref[:, pl.ds(kb0, MOBA_BLOCK)], p_ref[g], preferred_element_type=F32)
            pv = d if pv is None else pv + d
        if weight is not None:
            pv = pv * weight
        return alpha * acc + pv

    prev_top = top_block - ATTN_TILE_BLOCKS * jnp.where(step == n_tiles, 0, 1)
    prev_acc = stage_values(groups_of(prev_top) - 1, acc_ref[...], alpha_ref[...], prev_top)
    o_ref[...] = (prev_acc / l_ref[...]).T.astype(o_ref.dtype)

    stage_scores(0, causal=True)

    gate = lax.dot_general(kmean_ref[...].astype(BF16), q, nt, preferred_element_type=F32)
    blk = lax.broadcasted_iota(jnp.int32, gate.shape, 0)
    q_lane = lax.broadcasted_iota(jnp.int32, gate.shape, 1)
    own = tile * ATTN_TILE_BLOCKS + sum(
        (q_lane >= j * MOBA_BLOCK).astype(jnp.int32) for j in range(1, ATTN_TILE_BLOCKS))
    blk_f = blk.astype(F32)
    gate = jnp.where(blk < own, gate, neg_inf)
    selected = blk == own
    for j in range(min(MOBA_TOPK, n_blocks)):
        top = jnp.max(gate, axis=0, keepdims=True)
        first = jnp.min(jnp.where(gate == top, blk_f, float(n_blocks)), axis=0, keepdims=True)
        hit = blk_f == first
        selected = selected | (hit & (own > j))
        gate = jnp.where(hit, neg_inf, gate)
    bias_ref[0:n_blocks, :] = jnp.where(selected, 0.0, neg_inf)
    bias_ref[n_blocks:, :] = jnp.full((bias_ref.shape[0] - n_blocks, n_q), neg_inf, F32)

    def bias_row(group, g):
        b = block_of(group, g)
        return bias_ref[pl.ds(jnp.where(b < 0, n_blocks, b), 1), :]

    def trip(i, carry):
        m, l, acc, alpha_prev = carry
        acc = stage_values(i - 1, acc, alpha_prev, weight=jnp.where(i > 0, 1.0, 0.0))

        rows = [bias_row(i, g) for g in range(ATTN_GROUP)]
        m_new = m
        for g in range(ATTN_GROUP):
            m_new = jnp.maximum(m_new, mx_ref[g:g + 1, :] + rows[g])
        alpha = jnp.exp2(m - m_new)
        l = alpha * l
        for g in range(ATTN_GROUP):
            p = jnp.exp2(s_ref[g] - (m_new - rows[g]))
            l = l + jnp.sum(p, axis=0, keepdims=True)
            p_ref[g] = p.astype(BF16)

        stage_scores(i + 1, causal=False)
        return m_new, l, acc, alpha

    n_groups = jnp.where(step == n_tiles, 0, groups_of(top_block))
    init = (jnp.full((1, n_q), neg_inf, F32), jnp.zeros((1, n_q), F32),
            jnp.zeros((HEAD_DIM, n_q), F32), jnp.ones((1, n_q), F32))
    m, l, acc, alpha = lax.fori_loop(0, n_groups, trip, init)
    acc_ref[...] = acc
    l_ref[...] = jnp.where(l > 0.0, l, 1.0)
    alpha_ref[...] = alpha


def _attention(q, k, vt):
    seq, width = q.shape
    n_heads = width // HEAD_DIM
    n_blocks = seq // MOBA_BLOCK
    n_q = ATTN_TILE_BLOCKS * MOBA_BLOCK
    bias_rows = n_blocks + SUBLANES
    n_tiles = seq // n_q
    return pl.pallas_call(
        _attn_body,
        out_shape=jax.ShapeDtypeStruct((seq, width), BF16),
        grid=(n_heads, n_tiles + 1),
        in_specs=[
            pl.BlockSpec((n_q, HEAD_DIM), lambda hd, t: (jnp.minimum(t, n_tiles - 1), hd)),
            pl.BlockSpec((seq, HEAD_DIM), lambda hd, t: (0, hd)),
            pl.BlockSpec((HEAD_DIM, seq), lambda hd, t: (hd, 0)),
        ],
        out_specs=pl.BlockSpec((n_q, HEAD_DIM), lambda hd, t: (jnp.maximum(t - 1, 0), hd)),
        scratch_shapes=[pltpu.VMEM((n_blocks, HEAD_DIM), F32),
                        pltpu.VMEM((bias_rows, n_q), F32),
                        pltpu.VMEM((ATTN_GROUP, MOBA_BLOCK, n_q), F32),
                        pltpu.VMEM((ATTN_GROUP, n_q), F32),
                        pltpu.VMEM((ATTN_GROUP, MOBA_BLOCK, n_q), BF16),
                        pltpu.VMEM((HEAD_DIM, n_q), F32),
                        pltpu.VMEM((1, n_q), F32),
                        pltpu.VMEM((1, n_q), F32)],
        compiler_params=_params("arbitrary", "arbitrary"),
        name="moba_attn",
    )(q, k, vt)


def _moba_heads(h, gain, w_qkv, q_gain, k_gain):
    q, k, vt = _qkv(h, gain, w_qkv, q_gain, k_gain)
    return _attention(q, k, vt)


def kernel(x, ffn_norm, ffn_w_gate, ffn_w_up, ffn_w_down, mix_norm, s5_w_in, s5_a_re, s5_a_im, s5_log_dt, s5_b_re, s5_b_im, s5_c_re, s5_c_im, s5_d, s5_w_glu, s5_w_out, moba_w_qkv, moba_q_norm, moba_k_norm, moba_w_out):
    bsz, seq, d = x.shape
    depth = ffn_norm.shape[0]
    d_ff = ffn_w_gate.shape[-1]
    assert seq % (ATTN_TILE_BLOCKS * MOBA_BLOCK) == 0 and seq % FFN_ROWS == 0
    assert seq % min(SSM_ROWS, seq) == 0 and seq % PROJ_ROWS == 0
    assert d % LANES == 0 and d_ff % FFN_CHUNK == 0
    assert s5_a_re.shape[1:] == (d // S5_GROUP, S5_STATE)
    assert moba_w_qkv.shape[-1] == 3 * d and d % HEAD_DIM == 0
    wg_all = ffn_w_gate.reshape(depth * 2, d, d_ff)
    wu_all = ffn_w_up.reshape(depth * 2, d, d_ff)
    wd_all = ffn_w_down.reshape(depth * 2, d_ff, d)
    outs = []
    for b in range(bsz):
        h = (x.reshape(seq, d) if bsz == 1 else x[b]).astype(F32)
        for layer in range(depth):
            i = layer // 2
            ffn = functools.partial(_ffn, w_gate=wg_all, w_up=wu_all, w_down=wd_all)
            if layer % 2 == 0:
                h, u = ffn(h, ffn_norm[layer, 0], idx=2 * layer,
                           post=(mix_norm[layer], s5_w_in[i]))
                h = _s5_mixer(h, u, s5_a_re[i], s5_a_im[i], s5_log_dt[i], s5_b_re[i],
                              s5_b_im[i], s5_c_re[i], s5_c_im[i], s5_d[i], s5_w_glu[i],
                              s5_w_out[i])
                h = ffn(h, ffn_norm[layer, 1], idx=2 * layer + 1)
            else:
                h = ffn(h, ffn_norm[layer, 0], idx=2 * layer)
                o = _moba_heads(h, mix_norm[layer], moba_w_qkv[i], moba_q_norm[i],
                                moba_k_norm[i])
                h = ffn(h, ffn_norm[layer, 1], idx=2 * layer + 1, pre=(o, moba_w_out[i]))
        outs.append(h)
    out = outs[0].reshape(1, seq, d) if bsz == 1 else jnp.stack(outs)
    return out.astype(x.dtype)
```
